```python
import math
import jax
import jax.numpy as jnp
from jax import lax
import numpy as np

D_MODEL = 1024
BATCH = 16
SEQ = 256
DEPTH = 2
DEC_BATCH = 4
DEC_SEQ = 4096
PAST_LEN = 256

GRID_W = 64
N_MIXERS = 2
N_GLA_LAYERS = (DEPTH + 1) // 2
N_MLA_LAYERS = DEPTH // 2
D_FF = 2816
MACARON_W = 0.5
N_MOD = 9
EPS = 1e-6

GLA_HEADS = 4
GLA_DK = D_MODEL // 2 // GLA_HEADS
GLA_DV = D_MODEL // GLA_HEADS
GLA_GATE_RANK = 16
GLA_TAU = 16.0
GLA_CHUNK = 64
GLA_QK = GLA_HEADS * GLA_DK
GLA_VV = GLA_HEADS * GLA_DV
GLA_SPLITS = [GLA_QK, 2 * GLA_QK, 2 * GLA_QK + GLA_VV, 2 * GLA_QK + 2 * GLA_VV]
GLA_IN = 2 * GLA_QK + 2 * GLA_VV + 2 * GLA_GATE_RANK

MLA_HEADS = 16
MLA_NOPE = 128
MLA_ROPE = 64
MLA_V = 128
MLA_Q_RANK = 512
MLA_KV_RANK = 256
MLA_IN = MLA_Q_RANK + MLA_KV_RANK + MLA_ROPE
MLA_SCALE = (MLA_NOPE + MLA_ROPE) ** -0.5
Q_BLOCK = 128
ROPE_BASE = 10000.0

kernel_name = 'hybrid_gla_mla_prefix_diffusion_step'


def rms_norm(x, g):
    xf = x.astype(jnp.float32)
    y = xf * lax.rsqrt(jnp.mean(xf * xf, axis=-1, keepdims=True) + EPS)
    return (y * g.astype(jnp.float32)).astype(x.dtype)


def swiglu(h, w_in, w_out):
    g, u = jnp.split(h @ w_in, 2, axis=-1)
    return (jax.nn.silu(g) * u) @ w_out


def modulation(cond, w_mod, b_mod):
    m = jax.nn.silu(cond) @ w_mod + b_mod
    return jnp.split(m[..., None, :], N_MOD, axis=-1)


def axial_rope_tables(n_tokens):
    rows = n_tokens // GRID_W
    row = jnp.repeat(jnp.arange(rows), GRID_W).astype(jnp.float32)
    col = jnp.tile(jnp.arange(GRID_W), rows).astype(jnp.float32)
    n_pairs = MLA_ROPE // 4
    inv = ROPE_BASE ** (-jnp.arange(n_pairs, dtype=jnp.float32) / n_pairs)
    ang = jnp.stack([row[:, None] * inv, col[:, None] * inv], axis=1)
    return jnp.cos(ang), jnp.sin(ang)


def apply_axial_rope(x, cos, sin):
    xs = x.reshape(x.shape[:-1] + (2, 2, MLA_ROPE // 4)).astype(jnp.float32)
    x1, x2 = xs[..., 0, :], xs[..., 1, :]
    out = jnp.stack([x1 * cos - x2 * sin, x2 * cos + x1 * sin], axis=-2)
    return out.reshape(x.shape).astype(x.dtype)


def gla_scan(q, k, v, log_a, h0):
    bsz, t = q.shape[0], q.shape[1]
    n = t // GLA_CHUNK

    def chunks(a):
        return jnp.moveaxis(a.reshape((bsz, n, GLA_CHUNK) + a.shape[2:]), 1, 0)

    lower = jnp.tril(jnp.ones((GLA_CHUNK, GLA_CHUNK), bool))[None, :, :, None, None]

    def step(h, inp):
        qc, kc, vc, gc = inp
        b = jnp.cumsum(gc, axis=1)
        diff = jnp.where(lower, b[:, :, None] - b[:, None, :], -jnp.inf)
        attn = jnp.einsum('bijhd,bjhd->bhij', qc[:, :, None] * jnp.exp(diff), kc)
        o = (jnp.einsum('bhij,bjhv->bihv', attn, vc)
             + jnp.einsum('bihd,bhdv->bihv', qc * jnp.exp(b), h))
        b_last = b[:, -1]
        h = (jnp.exp(b_last)[..., None] * h
             + jnp.einsum('bjhd,bjhv->bhdv', kc * jnp.exp(b_last[:, None] - b), vc))
        return h, o

    h_final, o = lax.scan(step, h0, (chunks(q), chunks(k), chunks(v), chunks(log_a)))
    return jnp.moveaxis(o, 0, 1).reshape(v.shape), h_final


def gla_mixer(h, w_in, w_gate, b_gate, g_out, w_out, state0):
    bsz, t = h.shape[0], h.shape[1]
    f32 = jnp.float32
    q, k, v, r, z = jnp.split(h @ w_in, GLA_SPLITS, axis=-1)
    q = q.reshape(bsz, t, GLA_HEADS, GLA_DK).astype(f32) * (GLA_DK ** -0.5)
    k = k.reshape(bsz, t, GLA_HEADS, GLA_DK).astype(f32)
    v = v.reshape(bsz, t, GLA_HEADS, GLA_DV).astype(f32)
    z = z.reshape(bsz, t, 2, GLA_GATE_RANK)
    logit = jnp.einsum('btdr,drk->btdk', z, w_gate) + b_gate
    log_a = (jax.nn.log_sigmoid(logit.astype(f32)) / GLA_TAU).reshape(bsz, t, 2, GLA_HEADS, GLA_DK)
    o_f, s_f = gla_scan(q, k, v, log_a[:, :, 0], state0[:, 0])
    o_b, s_b = gla_scan(q[:, ::-1], k[:, ::-1], v[:, ::-1], log_a[:, ::-1, 1], state0[:, 1])
    o = rms_norm(o_f + o_b[:, ::-1], g_out.reshape(GLA_HEADS, GLA_DV))
    o = o.reshape(bsz, t, GLA_VV).astype(h.dtype) * jax.nn.silu(r)
    return o @ w_out, jnp.stack([s_f, s_b], axis=1)


def mla_project(h, w_in, g_q, g_kv):
    cq, ckv, kr = jnp.split(h @ w_in, [MLA_Q_RANK, MLA_Q_RANK + MLA_KV_RANK], axis=-1)
    return rms_norm(cq, g_q), rms_norm(ckv, g_kv), kr


def mla_heads(cq, ckv_all, w_uq, w_ukv):
    bsz, tq = cq.shape[0], cq.shape[1]
    tk = ckv_all.shape[1]
    q = (cq @ w_uq).reshape(bsz, tq, MLA_HEADS, MLA_NOPE + MLA_ROPE)
    kv = (ckv_all @ w_ukv).reshape(bsz, tk, MLA_HEADS, MLA_NOPE + MLA_V)
    return q[..., :MLA_NOPE], q[..., MLA_NOPE:], kv[..., :MLA_NOPE], kv[..., MLA_NOPE:]


def mla_attention(q_nope, q_rope, k_nope, k_rope, v):
    bsz, tq = q_nope.shape[0], q_nope.shape[1]
    nb = tq // Q_BLOCK

    def blocks(a):
        return jnp.moveaxis(a.reshape((bsz, nb, Q_BLOCK) + a.shape[2:]), 1, 0)

    def one_block(qs):
        qn, qr = qs
        s = jnp.einsum('bqhd,bkhd->bhqk', qn, k_nope) + jnp.einsum('bqhr,bkr->bhqk', qr, k_rope)
        p = jax.nn.softmax(s.astype(jnp.float32) * MLA_SCALE, axis=-1).astype(v.dtype)
        return jnp.einsum('bhqk,bkhv->bqhv', p, v)

    o = lax.map(one_block, (blocks(q_nope), blocks(q_rope)))
    return jnp.moveaxis(o, 0, 1).reshape(bsz, tq, MLA_HEADS * MLA_V)


def mla_context(h, w_in, g_q, g_kv, w_uq, w_ukv, w_out):
    cq, ckv, kr = mla_project(h, w_in, g_q, g_kv)
    qn, qr, kn, v = mla_heads(cq, ckv, w_uq, w_ukv)
    return mla_attention(qn, qr, kn, kr, v) @ w_out, (ckv, kr)


def mla_latent(h, ckv_ctx, kr_ctx, w_in, g_q, g_kv, w_uq, w_ukv, w_out):
    cq, ckv, kr = mla_project(h, w_in, g_q, g_kv)
    cos, sin = axial_rope_tables(h.shape[1])
    ckv_all = jnp.concatenate([ckv_ctx.astype(ckv.dtype), ckv], axis=1)
    kr_all = jnp.concatenate([kr_ctx.astype(kr.dtype), apply_axial_rope(kr, cos, sin)], axis=1)
    qn, qr, kn, v = mla_heads(cq, ckv_all, w_uq, w_ukv)
    qr = apply_axial_rope(qr, cos[:, None], sin[:, None])
    return mla_attention(qn, qr, kn, kr_all, v) @ w_out


def run_layer(x, mods, g_norm, w_ffn_in, w_ffn_out, mixer):
    def sub(x, i, fn, weight):
        shift, scale, gate = mods[3 * i], mods[3 * i + 1], mods[3 * i + 2]
        h = rms_norm(x, g_norm[i, 0]) * (1.0 + scale) + shift
        y, aux = fn(h)
        return x + weight * gate * rms_norm(y, g_norm[i, 1]), aux

    x, _ = sub(x, 0, lambda h: (swiglu(h, w_ffn_in[0], w_ffn_out[0]), None), MACARON_W)
    x, aux = sub(x, 1, mixer, 1.0)
    x, _ = sub(x, 2, lambda h: (swiglu(h, w_ffn_in[1], w_ffn_out[1]), None), MACARON_W)
    return x, aux


def setup_inputs(seed: int = 0) -> dict:
    key = jax.random.key(seed)
    ks = jax.random.split(key, 32)
    f32 = jnp.float32

    def nrm(k, shape, fan_in, gain=1.0):
        return jax.random.normal(k, shape, f32) * (gain * fan_in ** -0.5)

    def gains(k, shape):
        return 1.0 + 0.05 * jax.random.normal(k, shape, f32)

    D = D_MODEL
    return {
        'x_prompt': jax.random.normal(ks[0], (BATCH, SEQ, D), f32),
        'x_sample': jax.random.normal(ks[1], (DEC_BATCH, DEC_SEQ, D), f32),
        'state_gla': jax.random.normal(ks[2], (DEC_BATCH, N_GLA_LAYERS, 2, GLA_HEADS, GLA_DK, GLA_DV), f32),
        'cache_mla_ckv': jax.random.normal(ks[3], (DEC_BATCH, N_MLA_LAYERS, PAST_LEN, MLA_KV_RANK), f32),
        'cache_mla_krope': jax.random.normal(ks[4], (DEC_BATCH, N_MLA_LAYERS, PAST_LEN, MLA_ROPE), f32),
        'c': jax.random.normal(ks[5], (DEC_BATCH, D), f32),
        'c_ctx': jax.random.normal(ks[6], (D,), f32),
        'w_mod': nrm(ks[7], (DEPTH, D, N_MOD * D), D, 0.5),
        'b_mod': 0.02 * jax.random.normal(ks[8], (DEPTH, N_MOD * D), f32),
        'g_norm': gains(ks[9], (DEPTH, 3, 2, D)),
        'w_ffn_in': nrm(ks[10], (DEPTH, 2, D, 2 * D_FF), D),
        'w_ffn_out': nrm(ks[11], (DEPTH, 2, D_FF, D), D_FF),
        'gla_w_in': nrm(ks[12], (N_GLA_LAYERS, D, GLA_IN), D),
        'gla_w_gate': nrm(ks[13], (N_GLA_LAYERS, 2, GLA_GATE_RANK, GLA_QK), GLA_GATE_RANK),
        'gla_b_gate': 0.1 * jax.random.normal(ks[14], (N_GLA_LAYERS, 2, GLA_QK), f32),
        'gla_g_out': gains(ks[15], (N_GLA_LAYERS, GLA_VV)),
        'gla_w_out': nrm(ks[16], (N_GLA_LAYERS, GLA_VV, D), GLA_VV),
        'mla_w_in': nrm(ks[17], (N_MLA_LAYERS, D, MLA_IN), D),
        'mla_g_q': gains(ks[18], (N_MLA_LAYERS, MLA_Q_RANK)),
        'mla_g_kv': gains(ks[19], (N_MLA_LAYERS, MLA_KV_RANK)),
        'mla_w_uq': nrm(ks[20], (N_MLA_LAYERS, MLA_Q_RANK, MLA_HEADS * (MLA_NOPE + MLA_ROPE)), MLA_Q_RANK),
        'mla_w_ukv': nrm(ks[21], (N_MLA_LAYERS, MLA_KV_RANK, MLA_HEADS * (MLA_NOPE + MLA_V)), MLA_KV_RANK),
        'mla_w_out': nrm(ks[22], (N_MLA_LAYERS, MLA_HEADS * MLA_V, D), MLA_HEADS * MLA_V),
    }


def reference(x_prompt, x_sample, state_gla, cache_mla_ckv, cache_mla_krope, c, c_ctx,
              w_mod, b_mod, g_norm, w_ffn_in, w_ffn_out,
              gla_w_in, gla_w_gate, gla_b_gate, gla_g_out, gla_w_out,
              mla_w_in, mla_g_q, mla_g_kv, mla_w_uq, mla_w_ukv, mla_w_out):
    f32 = jnp.float32
    yp, ys = x_prompt, x_sample
    gla_states, mla_ckvs, mla_krs = [], [], []
    for l in range(DEPTH):
        mods_ctx = modulation(c_ctx, w_mod[l], b_mod[l])
        mods_lat = modulation(c, w_mod[l], b_mod[l])
        j = l // N_MIXERS
        if l % N_MIXERS == 0:
            p = (gla_w_in[j], gla_w_gate[j], gla_b_gate[j], gla_g_out[j], gla_w_out[j])
            zeros = jnp.zeros((yp.shape[0], 2, GLA_HEADS, GLA_DK, GLA_DV), f32)
            yp, st = run_layer(yp, mods_ctx, g_norm[l], w_ffn_in[l], w_ffn_out[l],
                               lambda h: gla_mixer(h, *p, zeros))
            gla_states.append(st)
            s0 = state_gla[:, j].astype(f32)
            ys, _ = run_layer(ys, mods_lat, g_norm[l], w_ffn_in[l], w_ffn_out[l],
                              lambda h: (gla_mixer(h, *p, s0)[0], None))
        else:
            p = (mla_w_in[j], mla_g_q[j], mla_g_kv[j], mla_w_uq[j], mla_w_ukv[j], mla_w_out[j])
            yp, (ckv, kr) = run_layer(yp, mods_ctx, g_norm[l], w_ffn_in[l], w_ffn_out[l],
                                      lambda h: mla_context(h, *p))
            mla_ckvs.append(ckv)
            mla_krs.append(kr)
            ckv_ctx, kr_ctx = cache_mla_ckv[:, j], cache_mla_krope[:, j]
            ys, _ = run_layer(ys, mods_lat, g_norm[l], w_ffn_in[l], w_ffn_out[l],
                              lambda h: (mla_latent(h, ckv_ctx, kr_ctx, *p), None))
    new_state_gla = jnp.stack(gla_states, axis=1).astype(x_prompt.dtype)
    new_cache_mla_ckv = jnp.stack(mla_ckvs, axis=1)
    new_cache_mla_krope = jnp.stack(mla_krs, axis=1)
    return (yp, ys, new_state_gla, new_cache_mla_ckv, new_cache_mla_krope)
```

```python
import functools

import jax
import jax.numpy as jnp
from jax import lax
from jax.experimental import pallas as pl
from jax.experimental.pallas import tpu as pltpu

F32 = jnp.float32
BF16 = jnp.bfloat16

D_MODEL = 1024
DEPTH = 2
GRID_W = 64
D_FF = 2816
MACARON_W = 0.5
N_MOD = 9
EPS = 1e-6

GLA_HEADS = 4
GLA_DK = 128
GLA_DV = 256
GLA_GATE_RANK = 16
GLA_TAU = 16.0
GLA_QK = GLA_HEADS * GLA_DK
GLA_VV = GLA_HEADS * GLA_DV

MLA_HEADS = 16
MLA_NOPE = 128
MLA_ROPE = 64
MLA_V = 128
MLA_Q_RANK = 512
MLA_KV_RANK = 256
MLA_SCALE = (MLA_NOPE + MLA_ROPE) ** -0.5
ROPE_BASE = 10000.0

VMEM_LIMIT_BYTES = 56 * 1024 * 1024

TOKEN_TILE = 512
FFN_CHUNK = 1408
GLA_CHUNK = 64
GLA_BLOCK = 256
MLA_Q_TILE = 512


def _params(*sem):
    return pltpu.CompilerParams(dimension_semantics=sem, vmem_limit_bytes=VMEM_LIMIT_BYTES)


def _resident(shape):
    nd = len(shape)
    return pl.BlockSpec(shape, lambda *_: (0,) * nd, pipeline_mode=pl.Buffered(1))


def _silu(x):
    return x * (1.0 / (1.0 + jnp.exp(-x)))


def _rms(x, g):
    return x * lax.rsqrt(jnp.mean(x * x, axis=-1, keepdims=True) + EPS) * g


def _pre(x, mod_ref, gn_ref, s):
    return _rms(x, gn_ref[0:1, :]) * (1.0 + mod_ref[3 * s + 1]) + mod_ref[3 * s]


def _residual(x, y, mod_ref, gn_ref, s, weight):
    return x + (weight * mod_ref[3 * s + 2]) * _rms(y, gn_ref[1:2, :])


def _dot(a, b):
    return jnp.dot(a, b, preferred_element_type=F32)


def _dot_nt(a, b):
    return lax.dot_general(a, b, (((1,), (1,)), ((), ())), preferred_element_type=F32)


def _dot_tn(a, b):
    return lax.dot_general(a, b, (((0,), (0,)), ((), ())), preferred_element_type=F32)


def _mod_kernel(cond_ref, w_ref, b_ref, o_ref):
    a = _silu(cond_ref[...]).astype(BF16)
    o_ref[...] = _dot(a, w_ref[...].astype(BF16)) + b_ref[...]


def _modulation(cond8, w_mod, b_mod):
    out = pl.pallas_call(
        _mod_kernel,
        grid=(DEPTH, N_MOD),
        in_specs=[
            pl.BlockSpec((8, D_MODEL), lambda l, j: (0, 0)),
            pl.BlockSpec((None, D_MODEL, D_MODEL), lambda l, j: (l, 0, j)),
            pl.BlockSpec((None, None, 1, D_MODEL), lambda l, j: (l, j, 0, 0)),
        ],
        out_specs=pl.BlockSpec((None, None, 8, D_MODEL), lambda l, j: (l, j, 0, 0)),
        out_shape=jax.ShapeDtypeStruct((DEPTH, N_MOD, 8, D_MODEL), F32),
        compiler_params=_params("parallel", "parallel"),
        name="modulation",
    )(cond8, w_mod, b_mod.reshape(DEPTH, N_MOD, 1, D_MODEL))
    return out.reshape(DEPTH, N_MOD, 8, 1, D_MODEL)


class _Stream:
    def __init__(self, batch, seq, mod_row0, per_batch_mod):
        self.batch, self.seq = batch, seq
        self.n = batch * seq
        self.tiles = self.n // TOKEN_TILE
        tiles_per_batch = seq // TOKEN_TILE
        if per_batch_mod:
            self.mod_row = lambda i: mod_row0 + i // tiles_per_batch
        else:
            self.mod_row = lambda i: mod_row0

    def tok(self, width):
        return pl.BlockSpec((TOKEN_TILE, width), lambda i: (i, 0))

    def mod(self):
        return pl.BlockSpec((N_MOD, None, 1, D_MODEL), lambda i: (0, self.mod_row(i), 0, 0))


def _ffn_kernel(x_ref, mod_ref, gn_ref, win_ref, wout_ref, o_ref, *, s):
    x = x_ref[...]
    h = _pre(x, mod_ref, gn_ref, s).astype(BF16)
    y = jnp.zeros((TOKEN_TILE, D_MODEL), F32)
    for lo in range(0, D_FF, FFN_CHUNK):
        g = _dot(h, win_ref[:, lo:lo + FFN_CHUNK])
        u = _dot(h, win_ref[:, D_FF + lo:D_FF + lo + FFN_CHUNK])
        a = (_silu(g) * u).astype(BF16)
        y = y + _dot(a, wout_ref[lo:lo + FFN_CHUNK, :])
    o_ref[...] = _residual(x, y, mod_ref, gn_ref, s, MACARON_W)


def _ffn(st, x, mods, gn, w_in, w_out, s):
    return pl.pallas_call(
        functools.partial(_ffn_kernel, s=s),
        grid=(st.tiles,),
        in_specs=[st.tok(D_MODEL), st.mod(), _resident((2, D_MODEL)),
                  _resident((D_MODEL, 2 * D_FF)), _resident((D_FF, D_MODEL))],
        out_specs=st.tok(D_MODEL),
        out_shape=jax.ShapeDtypeStruct((st.n, D_MODEL), F32),
        compiler_params=_params("parallel"),
        name="ffn",
    )(x, mods, gn, w_in, w_out)


def _gla_proj_kernel(x_ref, mod_ref, gn_ref, w_ref, wz_ref, wg_ref, bg_ref,
                     q_ref, k_ref, v_ref, r_ref, la_ref):
    h = _pre(x_ref[...], mod_ref, gn_ref, 1).astype(BF16)
    q_ref[...] = _dot(h, w_ref[:, 0:GLA_QK]) * (GLA_DK ** -0.5)
    k_ref[...] = _dot(h, w_ref[:, GLA_QK:2 * GLA_QK])
    v_ref[...] = _dot(h, w_ref[:, 2 * GLA_QK:2 * GLA_QK + GLA_VV]).astype(BF16)
    r_ref[...] = _dot(h, w_ref[:, 2 * GLA_QK + GLA_VV:2 * GLA_QK + 2 * GLA_VV])
    z = _dot(h, wz_ref[...]).astype(BF16)
    logit = _dot(z, wg_ref[...]) + bg_ref[...]
    la_ref[...] = (jnp.minimum(logit, 0.0) - jnp.log(1.0 + jnp.exp(-jnp.abs(logit)))) * (1.0 / GLA_TAU)


def _gla_proj(st, x, mods, gn, w_qkvr, w_z, w_gate_bd, b_gate):
    n = st.n
    return pl.pallas_call(
        _gla_proj_kernel,
        grid=(st.tiles,),
        in_specs=[st.tok(D_MODEL), st.mod(), _resident((2, D_MODEL)),
                  _resident(w_qkvr.shape), _resident(w_z.shape),
                  _resident(w_gate_bd.shape), _resident(b_gate.shape)],
        out_specs=[st.tok(GLA_QK), st.tok(GLA_QK), st.tok(GLA_VV), st.tok(GLA_VV), st.tok(2 * GLA_QK)],
        out_shape=[jax.ShapeDtypeStruct((n, GLA_QK), F32), jax.ShapeDtypeStruct((n, GLA_QK), F32),
                   jax.ShapeDtypeStruct((n, GLA_VV), BF16), jax.ShapeDtypeStruct((n, GLA_VV), F32),
                   jax.ShapeDtypeStruct((n, 2 * GLA_QK), F32)],
        compiler_params=_params("parallel"),
        name="gla_proj",
    )(x, mods, gn, w_qkvr, w_z, w_gate_bd, b_gate)


def _gla_scan_kernel(*refs, has_state, want_state):
    it = iter(refs)
    q_ref, k_ref, v_ref, la_ref = next(it), next(it), next(it), next(it)
    s0_ref = next(it) if has_state else None
    o_ref = next(it)
    sout_ref = next(it) if want_state else None
    st_ref = next(it)

    d = pl.program_id(1)
    blk = pl.program_id(2)
    n_chunks = GLA_BLOCK // GLA_CHUNK

    @pl.when(blk == 0)
    def _():
        if has_state:
            st_ref[...] = s0_ref[...]
        else:
            st_ref[...] = jnp.zeros(st_ref.shape, F32)

    ii = lax.broadcasted_iota(jnp.int32, (GLA_CHUNK, GLA_CHUNK), 0)
    jj = lax.broadcasted_iota(jnp.int32, (GLA_CHUNK, GLA_CHUNK), 1)
    mask = (ii - jj) * (1 - 2 * d) >= 0
    csum = mask.astype(F32)

    def chunk(c, carry):
        ce = c + d * (n_chunks - 1 - 2 * c)
        rows = pl.ds(pl.multiple_of(ce * GLA_CHUNK, GLA_CHUNK), GLA_CHUNK)
        la = la_ref[rows, :]
        b = jnp.dot(csum, la, preferred_element_type=F32, precision=lax.Precision.HIGHEST)
        btot = jnp.sum(la, axis=0, keepdims=True)
        bm = b[GLA_CHUNK // 2:GLA_CHUNK // 2 + 1, :]
        q = q_ref[rows, :]
        k = k_ref[rows, :]
        v = v_ref[rows, :]
        qm = (q * jnp.exp(b - bm)).astype(BF16)
        km = (k * jnp.exp(bm - b)).astype(BF16)
        qb = (q * jnp.exp(b)).astype(BF16)
        kt = (k * jnp.exp(btot - b)).astype(BF16)
        dec = jnp.exp(btot)
        for h in range(GLA_HEADS):
            sk = slice(h * GLA_DK, (h + 1) * GLA_DK)
            sv = slice(h * GLA_DV, (h + 1) * GLA_DV)
            att = jnp.where(mask, _dot_nt(qm[:, sk], km[:, sk]), 0.0).astype(BF16)
            s_t = st_ref[h]
            o_ref[rows, sv] = _dot(att, v[:, sv]) + _dot_nt(qb[:, sk], s_t.astype(BF16))
            st_ref[h] = s_t * dec[:, sk] + _dot_tn(v[:, sv], kt[:, sk])
        return carry

    lax.fori_loop(0, n_chunks, chunk, 0)

    if want_state:
        @pl.when(blk == pl.num_programs(2) - 1)
        def _():
            sout_ref[...] = st_ref[...]


def _gla_scan(st, q, k, v, la, s0_t, want_state):
    nb = st.seq // GLA_BLOCK
    has_state = s0_t is not None

    def row(b, d, i):
        return b * nb + i + d * (nb - 1 - 2 * i)

    state_spec = pl.BlockSpec((None, None, GLA_HEADS, GLA_DV, GLA_DK), lambda b, d, i: (b, d, 0, 0, 0))
    in_specs = [pl.BlockSpec((GLA_BLOCK, GLA_QK), lambda b, d, i: (row(b, d, i), 0)),
                pl.BlockSpec((GLA_BLOCK, GLA_QK), lambda b, d, i: (row(b, d, i), 0)),
                pl.BlockSpec((GLA_BLOCK, GLA_VV), lambda b, d, i: (row(b, d, i), 0)),
                pl.BlockSpec((GLA_BLOCK, GLA_QK), lambda b, d, i: (row(b, d, i), d))]
    args = [q, k, v, la]
    if has_state:
        in_specs.append(state_spec)
        args.append(s0_t)
    out_specs = [pl.BlockSpec((None, GLA_BLOCK, GLA_VV), lambda b, d, i: (d, row(b, d, i), 0))]
    out_shape = [jax.ShapeDtypeStruct((2, st.n, GLA_VV), F32)]
    if want_state:
        out_specs.append(state_spec)
        out_shape.append(jax.ShapeDtypeStruct((st.batch, 2, GLA_HEADS, GLA_DV, GLA_DK), F32))
    return pl.pallas_call(
        functools.partial(_gla_scan_kernel, has_state=has_state, want_state=want_state),
        grid=(st.batch, 2, nb),
        in_specs=in_specs,
        out_specs=out_specs,
        out_shape=out_shape,
        scratch_shapes=[pltpu.VMEM((GLA_HEADS, GLA_DV, GLA_DK), F32)],
        compiler_params=_params("parallel", "parallel", "arbitrary"),
        name="gla_scan",
    )(*args)


def _gla_post_kernel(o_ref, r_ref, x_ref, mod_ref, gn_ref, go_ref, w_ref, out_ref):
    o = o_ref[0] + o_ref[1]
    heads = []
    for h in range(GLA_HEADS):
        sv = slice(h * GLA_DV, (h + 1) * GLA_DV)
        heads.append(_rms(o[:, sv], go_ref[:, sv]))
    g = (jnp.concatenate(heads, axis=1) * _silu(r_ref[...])).astype(BF16)
    out_ref[...] = _residual(x_ref[...], _dot(g, w_ref[...]), mod_ref, gn_ref, 1, 1.0)


def _gla_post(st, o, r, x, mods, gn, g_out, w_out):
    return pl.pallas_call(
        _gla_post_kernel,
        grid=(st.tiles,),
        in_specs=[pl.BlockSpec((2, TOKEN_TILE, GLA_VV), lambda i: (0, i, 0)),
                  st.tok(GLA_VV), st.tok(D_MODEL), st.mod(), _resident((2, D_MODEL)),
                  _resident((1, GLA_VV)), _resident((GLA_VV, D_MODEL))],
        out_specs=st.tok(D_MODEL),
        out_shape=jax.ShapeDtypeStruct((st.n, D_MODEL), F32),
        compiler_params=_params("parallel"),
        name="gla_post",
    )(o, r, x, mods, gn, g_out, w_out)


def _mla_proj_kernel(x_ref, mod_ref, gn_ref, w_ref, gq_ref, gkv_ref, cq_ref, ckv_ref, kr_ref):
    h = _pre(x_ref[...], mod_ref, gn_ref, 1).astype(BF16)
    p = _dot(h, w_ref[...])
    cq_ref[...] = _rms(p[:, 0:MLA_Q_RANK], gq_ref[...]).astype(BF16)
    ckv_ref[...] = _rms(p[:, MLA_Q_RANK:MLA_Q_RANK + MLA_KV_RANK], gkv_ref[...])
    kr_ref[...] = p[:, MLA_Q_RANK + MLA_KV_RANK:]


def _mla_proj(st, x, mods, gn, w_in_ext, g_q, g_kv):
    n = st.n
    return pl.pallas_call(
        _mla_proj_kernel,
        grid=(st.tiles,),
        in_specs=[st.tok(D_MODEL), st.mod(), _resident((2, D_MODEL)), _resident(w_in_ext.shape),
                  _resident((1, MLA_Q_RANK)), _resident((1, MLA_KV_RANK))],
        out_specs=[st.tok(MLA_Q_RANK), st.tok(MLA_KV_RANK), st.tok(2 * MLA_ROPE)],
        out_shape=[jax.ShapeDtypeStruct((n, MLA_Q_RANK), BF16),
                   jax.ShapeDtypeStruct((n, MLA_KV_RANK), F32),
                   jax.ShapeDtypeStruct((n, 2 * MLA_ROPE), F32)],
        compiler_params=_params("parallel"),
        name="mla_proj",
    )(x, mods, gn, w_in_ext, g_q, g_kv)


def _mla_attn_kernel(cq_ref, ckv_ref, kr_ref, tabk_ref, tabq_ref, wuq_ref, wukv_ref, o_ref,
                     k_scr, v_scr, *, heads):
    tk = k_scr.shape[1]

    @pl.when(pl.program_id(2) == 0)
    def _():
        y = kr_ref[...] * tabk_ref[...]
        kr = (y + pltpu.roll(y, MLA_ROPE, axis=1)).astype(BF16)
        ckv = ckv_ref[...]
        for h in range(heads):
            kv = _dot(ckv, wukv_ref[h])
            k_scr[h, :, 0:MLA_NOPE] = kv[:, 0:MLA_NOPE].astype(BF16)
            k_scr[h, :, MLA_NOPE:] = kr
            v_scr[h, :, 0:MLA_V] = kv[:, MLA_NOPE:].astype(BF16)
            v_scr[h, :, MLA_V:] = jnp.ones((tk, MLA_V), BF16)

    cq = cq_ref[...]
    tabq = tabq_ref[...] * MLA_SCALE
    for h in range(heads):
        q = _dot(cq, wuq_ref[h])
        qq = jnp.concatenate([q[:, 0:MLA_NOPE] * MLA_SCALE, q[:, MLA_NOPE:] * tabq], axis=1).astype(BF16)
        s = _dot_nt(qq, k_scr[h])
        p = jnp.exp(s - jnp.max(s, axis=1, keepdims=True)).astype(BF16)
        oe = _dot(p, v_scr[h])
        o_ref[:, h * MLA_V:(h + 1) * MLA_V] = (oe[:, 0:MLA_V] / oe[:, MLA_V:]).astype(BF16)


def _mla_attn(batch, tq_total, cq, ckv_all, kr_all, tabk, tabq, w_uq, w_ukv, heads, q_tile):
    tk = ckv_all.shape[1]
    nq = tq_total // q_tile
    return pl.pallas_call(
        functools.partial(_mla_attn_kernel, heads=heads),
        grid=(batch, MLA_HEADS // heads, nq),
        in_specs=[pl.BlockSpec((q_tile, MLA_Q_RANK), lambda b, g, i: (b * nq + i, 0)),
                  pl.BlockSpec((None, tk, MLA_KV_RANK), lambda b, g, i: (b, 0, 0)),
                  pl.BlockSpec((None, tk, 2 * MLA_ROPE), lambda b, g, i: (b, 0, 0)),
                  pl.BlockSpec((tk, 2 * MLA_ROPE), lambda b, g, i: (0, 0)),
                  pl.BlockSpec((q_tile, 2 * MLA_ROPE), lambda b, g, i: (i, 0)),
                  pl.BlockSpec((heads, MLA_Q_RANK, 256), lambda b, g, i: (g, 0, 0)),
                  pl.BlockSpec((heads, MLA_KV_RANK, 256), lambda b, g, i: (g, 0, 0))],
        out_specs=pl.BlockSpec((q_tile, heads * MLA_V), lambda b, g, i: (b * nq + i, g)),
        out_shape=jax.ShapeDtypeStruct((batch * tq_total, MLA_HEADS * MLA_V), BF16),
        scratch_shapes=[pltpu.VMEM((heads, tk, 256), BF16), pltpu.VMEM((heads, tk, 256), BF16)],
        compiler_params=_params("parallel", "parallel", "arbitrary"),
        name="mla_attn",
    )(cq, ckv_all, kr_all, tabk, tabq, w_uq, w_ukv)


def _mla_post_kernel(a_ref, x_ref, mod_ref, gn_ref, w_ref, out_ref):
    out_ref[...] = _residual(x_ref[...], _dot(a_ref[...], w_ref[...]), mod_ref, gn_ref, 1, 1.0)


def _mla_post(st, a, x, mods, gn, w_out):
    return pl.pallas_call(
        _mla_post_kernel,
        grid=(st.tiles,),
        in_specs=[st.tok(MLA_HEADS * MLA_V), st.tok(D_MODEL), st.mod(), _resident((2, D_MODEL)),
                  _resident((MLA_HEADS * MLA_V, D_MODEL))],
        out_specs=st.tok(D_MODEL),
        out_shape=jax.ShapeDtypeStruct((st.n, D_MODEL), F32),
        compiler_params=_params("parallel"),
        name="mla_post",
    )(a, x, mods, gn, w_out)


_ROPE_SWAP = tuple(list(range(16, 32)) + list(range(0, 16)) + list(range(48, 64)) + list(range(32, 48)))


def _rope_table(n_tokens):
    rows = n_tokens // GRID_W
    row = jnp.repeat(jnp.arange(rows), GRID_W).astype(F32)
    col = jnp.tile(jnp.arange(GRID_W), rows).astype(F32)
    n_pairs = MLA_ROPE // 4
    inv = ROPE_BASE ** (-jnp.arange(n_pairs, dtype=F32) / n_pairs)
    ar, ac = row[:, None] * inv, col[:, None] * inv
    cos = jnp.concatenate([jnp.cos(ar), jnp.cos(ar), jnp.cos(ac), jnp.cos(ac)], axis=1)
    sin = jnp.concatenate([-jnp.sin(ar), jnp.sin(ar), -jnp.sin(ac), jnp.sin(ac)], axis=1)
    return jnp.concatenate([cos, sin], axis=1)


def _identity_table(n_tokens):
    return jnp.concatenate([jnp.ones((n_tokens, MLA_ROPE), F32), jnp.zeros((n_tokens, MLA_ROPE), F32)], axis=1)


def kernel(x_prompt, x_sample, state_gla, cache_mla_ckv, cache_mla_krope, c, c_ctx, w_mod, b_mod, g_norm, w_ffn_in, w_ffn_out, gla_w_in, gla_w_gate, gla_b_gate, gla_g_out, gla_w_out, mla_w_in, mla_g_q, mla_g_kv, mla_w_uq, mla_w_ukv, mla_w_out):
    batch, seq = x_prompt.shape[0], x_prompt.shape[1]
    dec_batch, dec_seq = x_sample.shape[0], x_sample.shape[1]
    past = cache_mla_ckv.shape[2]
    ctx = _Stream(batch, seq, dec_batch, per_batch_mod=False)
    lat = _Stream(dec_batch, dec_seq, 0, per_batch_mod=True)

    cond8 = jnp.concatenate([c, c_ctx[None, :], jnp.zeros((8 - dec_batch - 1, D_MODEL), F32)], axis=0)
    mods = _modulation(cond8, w_mod, b_mod)

    xs = {ctx: x_prompt.reshape(ctx.n, D_MODEL), lat: x_sample.reshape(lat.n, D_MODEL)}
    w_ffn_in_b = w_ffn_in.astype(BF16)
    w_ffn_out_b = w_ffn_out.astype(BF16)
    gla_states, mla_ckvs, mla_krs = [], [], []

    for l in range(DEPTH):
        j = l // 2
        for st in (ctx, lat):
            xs[st] = _ffn(st, xs[st], mods[l], g_norm[l, 0], w_ffn_in_b[l, 0], w_ffn_out_b[l, 0], 0)

        if l % 2 == 0:
            w_in = gla_w_in[j].astype(BF16)
            w_qkvr, w_z = w_in[:, :2 * GLA_QK + 2 * GLA_VV], w_in[:, 2 * GLA_QK + 2 * GLA_VV:]
            zeros = jnp.zeros((GLA_GATE_RANK, GLA_QK), F32)
            w_gate_bd = jnp.concatenate([jnp.concatenate([gla_w_gate[j, 0], zeros], axis=1),
                                         jnp.concatenate([zeros, gla_w_gate[j, 1]], axis=1)], axis=0).astype(BF16)
            b_gate = gla_b_gate[j].reshape(1, 2 * GLA_QK)
            g_out = gla_g_out[j].reshape(1, GLA_VV)
            w_out = gla_w_out[j].astype(BF16)
            for st in (ctx, lat):
                q, k, v, r, la = _gla_proj(st, xs[st], mods[l], g_norm[l, 1], w_qkvr, w_z, w_gate_bd, b_gate)
                if st is ctx:
                    o, s_t = _gla_scan(st, q, k, v, la, None, True)
                    gla_states.append(jnp.swapaxes(s_t, -1, -2))
                else:
                    s0_t = jnp.swapaxes(state_gla[:, j].astype(F32), -1, -2)
                    (o,) = _gla_scan(st, q, k, v, la, s0_t, False)
                xs[st] = _gla_post(st, o, r, xs[st], mods[l], g_norm[l, 1], g_out, w_out)
        else:
            swap = jnp.array(_ROPE_SWAP)
            w_in = mla_w_in[j]
            w_in_ext = jnp.concatenate([w_in, w_in[:, MLA_Q_RANK + MLA_KV_RANK:][:, swap]], axis=1).astype(BF16)
            w_uq = mla_w_uq[j].reshape(MLA_Q_RANK, MLA_HEADS, MLA_NOPE + MLA_ROPE)
            w_uq = jnp.concatenate([w_uq, w_uq[:, :, MLA_NOPE:][:, :, swap]], axis=2)
            w_uq = jnp.transpose(w_uq, (1, 0, 2)).astype(BF16)
            w_ukv = jnp.transpose(mla_w_ukv[j].reshape(MLA_KV_RANK, MLA_HEADS, MLA_NOPE + MLA_V), (1, 0, 2)).astype(BF16)
            g_q = mla_g_q[j].reshape(1, MLA_Q_RANK)
            g_kv = mla_g_kv[j].reshape(1, MLA_KV_RANK)
            w_out = mla_w_out[j].astype(BF16)
            for st in (ctx, lat):
                cq, ckv, kr2 = _mla_proj(st, xs[st], mods[l], g_norm[l, 1], w_in_ext, g_q, g_kv)
                ckv3 = ckv.reshape(st.batch, st.seq, MLA_KV_RANK)
                kr3 = kr2.reshape(st.batch, st.seq, 2 * MLA_ROPE)
                if st is ctx:
                    mla_ckvs.append(ckv3)
                    mla_krs.append(kr3[:, :, :MLA_ROPE])
                    tab = _identity_table(st.seq)
                    a = _mla_attn(st.batch, st.seq, cq, ckv3.astype(BF16), kr3, tab, tab, w_uq, w_ukv,
                                  heads=MLA_HEADS, q_tile=st.seq)
                else:
                    ckv_all = jnp.concatenate([cache_mla_ckv[:, j].astype(F32), ckv3], axis=1).astype(BF16)
                    kr_ctx = cache_mla_krope[:, j].astype(F32)
                    kr_all = jnp.concatenate([jnp.concatenate([kr_ctx, jnp.zeros_like(kr_ctx)], axis=2), kr3], axis=1)
                    tabq = _rope_table(st.seq)
                    tabk = jnp.concatenate([_identity_table(past), tabq], axis=0)
                    a = _mla_attn(st.batch, st.seq, cq, ckv_all, kr_all, tabk, tabq, w_uq, w_ukv,
                                  heads=1, q_tile=MLA_Q_TILE)
                xs[st] = _mla_post(st, a, xs[st], mods[l], g_norm[l, 1], w_out)

        for st in (ctx, lat):
            xs[st] = _ffn(st, xs[st], mods[l], g_norm[l, 2], w_ffn_in_b[l, 1], w_ffn_out_b[l, 1], 2)

    yp = xs[ctx].reshape(batch, seq, D_MODEL)
    ys = xs[lat].reshape(dec_batch, dec_seq, D_MODEL)
    new_state_gla = jnp.stack(gla_states, axis=1).astype(x_prompt.dtype)
    new_cache_mla_ckv = jnp.stack(mla_ckvs, axis=1)
    new_cache_mla_krope = jnp.stack(mla_krs, axis=1)
    return (yp, ys, new_state_gla, new_cache_mla_ckv, new_cache_mla_krope)
```

```python
import functools

import jax
import jax.numpy as jnp
from jax import lax
from jax.experimental import pallas as pl
from jax.experimental.pallas import tpu as pltpu

F32 = jnp.float32
BF16 = jnp.bfloat16

D_MODEL = 1024
DEPTH = 2
GRID_W = 64
D_FF = 2816
MACARON_W = 0.5
N_MOD = 9
EPS = 1e-6

GLA_HEADS = 4
GLA_DK = 128
GLA_DV = 256
GLA_GATE_RANK = 16
GLA_TAU = 16.0
GLA_QK = GLA_HEADS * GLA_DK
GLA_VV = GLA_HEADS * GLA_DV

MLA_HEADS = 16
MLA_NOPE = 128
MLA_ROPE = 64
MLA_V = 128
MLA_Q_RANK = 512
MLA_KV_RANK = 256
MLA_SCALE = (MLA_NOPE + MLA_ROPE) ** -0.5
ROPE_BASE = 10000.0

VMEM_LIMIT_BYTES = 56 * 1024 * 1024

TOKEN_TILE = 512
FFN_CHUNK = 1408
GLA_CHUNK = 64
GLA_BLOCK = 256
MLA_Q_SUB = 256


def _params(*sem):
    return pltpu.CompilerParams(dimension_semantics=sem, vmem_limit_bytes=VMEM_LIMIT_BYTES)


def _resident(shape):
    nd = len(shape)
    return pl.BlockSpec(shape, lambda *_: (0,) * nd, pipeline_mode=pl.Buffered(1))


def _silu(x):
    return x * (1.0 / (1.0 + jnp.exp(-x)))


def _rms(x, g):
    return x * lax.rsqrt(jnp.mean(x * x, axis=-1, keepdims=True) + EPS) * g


def _pre(x, mod_ref, gn_ref, s):
    return _rms(x, gn_ref[0:1, :]) * (1.0 + mod_ref[3 * s + 1]) + mod_ref[3 * s]


def _residual(x, y, mod_ref, gn_ref, s, weight):
    return x + (weight * mod_ref[3 * s + 2]) * _rms(y, gn_ref[1:2, :])


def _dot(a, b):
    return jnp.dot(a, b, preferred_element_type=F32)


def _dot_nt(a, b):
    return lax.dot_general(a, b, (((1,), (1,)), ((), ())), preferred_element_type=F32)


def _dot_tn(a, b):
    return lax.dot_general(a, b, (((0,), (0,)), ((), ())), preferred_element_type=F32)


def _mod_kernel(cond_ref, w_ref, b_ref, o_ref):
    a = _silu(cond_ref[...]).astype(BF16)
    o_ref[...] = _dot(a, w_ref[...].astype(BF16)) + b_ref[...]


def _modulation(cond8, w_mod, b_mod):
    out = pl.pallas_call(
        _mod_kernel,
        grid=(DEPTH, N_MOD),
        in_specs=[
            pl.BlockSpec((8, D_MODEL), lambda l, j: (0, 0)),
            pl.BlockSpec((None, D_MODEL, D_MODEL), lambda l, j: (l, 0, j)),
            pl.BlockSpec((None, None, 1, D_MODEL), lambda l, j: (l, j, 0, 0)),
        ],
        out_specs=pl.BlockSpec((None, None, 8, D_MODEL), lambda l, j: (l, j, 0, 0)),
        out_shape=jax.ShapeDtypeStruct((DEPTH, N_MOD, 8, D_MODEL), F32),
        compiler_params=_params("parallel", "parallel"),
        name="modulation",
    )(cond8, w_mod, b_mod.reshape(DEPTH, N_MOD, 1, D_MODEL))
    return out.reshape(DEPTH, N_MOD, 8, 1, D_MODEL)


class _Stream:
    def __init__(self, batch, seq, mod_row0, per_batch_mod):
        self.batch, self.seq = batch, seq
        self.n = batch * seq
        self.tiles = self.n // TOKEN_TILE
        tiles_per_batch = seq // TOKEN_TILE
        if per_batch_mod:
            self.mod_row = lambda i: mod_row0 + i // tiles_per_batch
        else:
            self.mod_row = lambda i: mod_row0

    def tok(self, width):
        return pl.BlockSpec((TOKEN_TILE, width), lambda i: (i, 0))

    def mod(self):
        return pl.BlockSpec((N_MOD, None, 1, D_MODEL), lambda i: (0, self.mod_row(i), 0, 0))


def _ffn_kernel(x_ref, mod_ref, gn_ref, win_ref, wout_ref, o_ref, *, s):
    x = x_ref[...]
    h = _pre(x, mod_ref, gn_ref, s).astype(BF16)
    y = jnp.zeros((TOKEN_TILE, D_MODEL), F32)
    for lo in range(0, D_FF, FFN_CHUNK):
        g = _dot(h, win_ref[:, lo:lo + FFN_CHUNK])
        u = _dot(h, win_ref[:, D_FF + lo:D_FF + lo + FFN_CHUNK])
        a = (_silu(g) * u).astype(BF16)
        y = y + _dot(a, wout_ref[lo:lo + FFN_CHUNK, :])
    o_ref[...] = _residual(x, y, mod_ref, gn_ref, s, MACARON_W)


def _ffn(st, x, mods, gn, w_in, w_out, s):
    return pl.pallas_call(
        functools.partial(_ffn_kernel, s=s),
        grid=(st.tiles,),
        in_specs=[st.tok(D_MODEL), st.mod(), _resident((2, D_MODEL)),
                  _resident((D_MODEL, 2 * D_FF)), _resident((D_FF, D_MODEL))],
        out_specs=st.tok(D_MODEL),
        out_shape=jax.ShapeDtypeStruct((st.n, D_MODEL), F32),
        compiler_params=_params("parallel"),
        name="ffn",
    )(x, mods, gn, w_in, w_out)


def _gla_proj_kernel(x_ref, mod_ref, gn_ref, w_ref, wz_ref, wg_ref, bg_ref,
                     q_ref, k_ref, v_ref, r_ref, la_ref):
    h = _pre(x_ref[...], mod_ref, gn_ref, 1).astype(BF16)
    q_ref[...] = _dot(h, w_ref[:, 0:GLA_QK]) * (GLA_DK ** -0.5)
    k_ref[...] = _dot(h, w_ref[:, GLA_QK:2 * GLA_QK])
    v_ref[...] = _dot(h, w_ref[:, 2 * GLA_QK:2 * GLA_QK + GLA_VV]).astype(BF16)
    r_ref[...] = _dot(h, w_ref[:, 2 * GLA_QK + GLA_VV:2 * GLA_QK + 2 * GLA_VV])
    z = _dot(h, wz_ref[...]).astype(BF16)
    logit = _dot(z, wg_ref[...]) + bg_ref[...]
    la_ref[...] = (jnp.minimum(logit, 0.0) - jnp.log(1.0 + jnp.exp(-jnp.abs(logit)))) * (1.0 / GLA_TAU)


def _gla_proj(st, x, mods, gn, w_qkvr, w_z, w_gate_bd, b_gate):
    n = st.n
    return pl.pallas_call(
        _gla_proj_kernel,
        grid=(st.tiles,),
        in_specs=[st.tok(D_MODEL), st.mod(), _resident((2, D_MODEL)),
                  _resident(w_qkvr.shape), _resident(w_z.shape),
                  _resident(w_gate_bd.shape), _resident(b_gate.shape)],
        out_specs=[st.tok(GLA_QK), st.tok(GLA_QK), st.tok(GLA_VV), st.tok(GLA_VV), st.tok(2 * GLA_QK)],
        out_shape=[jax.ShapeDtypeStruct((n, GLA_QK), F32), jax.ShapeDtypeStruct((n, GLA_QK), F32),
                   jax.ShapeDtypeStruct((n, GLA_VV), BF16), jax.ShapeDtypeStruct((n, GLA_VV), F32),
                   jax.ShapeDtypeStruct((n, 2 * GLA_QK), F32)],
        compiler_params=_params("parallel"),
        name="gla_proj",
    )(x, mods, gn, w_qkvr, w_z, w_gate_bd, b_gate)


def _gla_scan_kernel(*refs, has_state, want_state):
    it = iter(refs)
    q_ref, k_ref, v_ref, la_ref = next(it), next(it), next(it), next(it)
    s0_ref = next(it) if has_state else None
    o_ref = next(it)
    sout_ref = next(it) if want_state else None
    st_ref = next(it)

    d = pl.program_id(1)
    blk = pl.program_id(2)
    n_chunks = GLA_BLOCK // GLA_CHUNK

    @pl.when(blk == 0)
    def _():
        if has_state:
            st_ref[...] = s0_ref[...]
        else:
            st_ref[...] = jnp.zeros(st_ref.shape, F32)

    ii = lax.broadcasted_iota(jnp.int32, (GLA_CHUNK, GLA_CHUNK), 0)
    jj = lax.broadcasted_iota(jnp.int32, (GLA_CHUNK, GLA_CHUNK), 1)
    mask = (ii - jj) * (1 - 2 * d) >= 0
    csum = mask.astype(F32)

    def chunk(c, carry):
        ce = c + d * (n_chunks - 1 - 2 * c)
        rows = pl.ds(pl.multiple_of(ce * GLA_CHUNK, GLA_CHUNK), GLA_CHUNK)
        la = la_ref[rows, :]
        b = jnp.dot(csum, la, preferred_element_type=F32, precision=lax.Precision.HIGHEST)
        btot = jnp.sum(la, axis=0, keepdims=True)
        bm = b[GLA_CHUNK // 2:GLA_CHUNK // 2 + 1, :]
        q = q_ref[rows, :]
        k = k_ref[rows, :]
        v = v_ref[rows, :]
        qm = (q * jnp.exp(b - bm)).astype(BF16)
        km = (k * jnp.exp(bm - b)).astype(BF16)
        qb = (q * jnp.exp(b)).astype(BF16)
        kt = (k * jnp.exp(btot - b)).astype(BF16)
        dec = jnp.exp(btot)
        for h in range(GLA_HEADS):
            sk = slice(h * GLA_DK, (h + 1) * GLA_DK)
            sv = slice(h * GLA_DV, (h + 1) * GLA_DV)
            att = jnp.where(mask, _dot_nt(qm[:, sk], km[:, sk]), 0.0).astype(BF16)
            s_t = st_ref[h]
            o_ref[rows, sv] = _dot(att, v[:, sv]) + _dot_nt(qb[:, sk], s_t.astype(BF16))
            st_ref[h] = s_t * dec[:, sk] + _dot_tn(v[:, sv], kt[:, sk])
        return carry

    lax.fori_loop(0, n_chunks, chunk, 0, unroll=True)

    if want_state:
        @pl.when(blk == pl.num_programs(2) - 1)
        def _():
            sout_ref[...] = st_ref[...]


def _gla_scan(st, q, k, v, la, s0_t, want_state):
    nb = st.seq // GLA_BLOCK
    has_state = s0_t is not None

    def row(b, d, i):
        return b * nb + i + d * (nb - 1 - 2 * i)

    state_spec = pl.BlockSpec((None, None, GLA_HEADS, GLA_DV, GLA_DK), lambda b, d, i: (b, d, 0, 0, 0))
    in_specs = [pl.BlockSpec((GLA_BLOCK, GLA_QK), lambda b, d, i: (row(b, d, i), 0)),
                pl.BlockSpec((GLA_BLOCK, GLA_QK), lambda b, d, i: (row(b, d, i), 0)),
                pl.BlockSpec((GLA_BLOCK, GLA_VV), lambda b, d, i: (row(b, d, i), 0)),
                pl.BlockSpec((GLA_BLOCK, GLA_QK), lambda b, d, i: (row(b, d, i), d))]
    args = [q, k, v, la]
    if has_state:
        in_specs.append(state_spec)
        args.append(s0_t)
    out_specs = [pl.BlockSpec((None, GLA_BLOCK, GLA_VV), lambda b, d, i: (d, row(b, d, i), 0))]
    out_shape = [jax.ShapeDtypeStruct((2, st.n, GLA_VV), F32)]
    if want_state:
        out_specs.append(state_spec)
        out_shape.append(jax.ShapeDtypeStruct((st.batch, 2, GLA_HEADS, GLA_DV, GLA_DK), F32))
    return pl.pallas_call(
        functools.partial(_gla_scan_kernel, has_state=has_state, want_state=want_state),
        grid=(st.batch, 2, nb),
        in_specs=in_specs,
        out_specs=out_specs,
        out_shape=out_shape,
        scratch_shapes=[pltpu.VMEM((GLA_HEADS, GLA_DV, GLA_DK), F32)],
        compiler_params=_params("parallel", "parallel", "arbitrary"),
        name="gla_scan",
    )(*args)


def _gla_post_kernel(o_ref, r_ref, x_ref, mod_ref, gn_ref, go_ref, w_ref, out_ref):
    o = o_ref[0] + o_ref[1]
    heads = []
    for h in range(GLA_HEADS):
        sv = slice(h * GLA_DV, (h + 1) * GLA_DV)
        heads.append(_rms(o[:, sv], go_ref[:, sv]))
    g = (jnp.concatenate(heads, axis=1) * _silu(r_ref[...])).astype(BF16)
    out_ref[...] = _residual(x_ref[...], _dot(g, w_ref[...]), mod_ref, gn_ref, 1, 1.0)


def _gla_post(st, o, r, x, mods, gn, g_out, w_out):
    return pl.pallas_call(
        _gla_post_kernel,
        grid=(st.tiles,),
        in_specs=[pl.BlockSpec((2, TOKEN_TILE, GLA_VV), lambda i: (0, i, 0)),
                  st.tok(GLA_VV), st.tok(D_MODEL), st.mod(), _resident((2, D_MODEL)),
                  _resident((1, GLA_VV)), _resident((GLA_VV, D_MODEL))],
        out_specs=st.tok(D_MODEL),
        out_shape=jax.ShapeDtypeStruct((st.n, D_MODEL), F32),
        compiler_params=_params("parallel"),
        name="gla_post",
    )(o, r, x, mods, gn, g_out, w_out)


def _mla_proj_kernel(x_ref, mod_ref, gn_ref, w_ref, gq_ref, gkv_ref, cqt_ref, ckv_ref, kr_ref):
    h = _pre(x_ref[...], mod_ref, gn_ref, 1).astype(BF16)
    p = _dot(h, w_ref[...])
    cqt = _rms(p[:, 0:MLA_Q_RANK], gq_ref[...]).T.astype(BF16)
    for j in range(TOKEN_TILE // MLA_Q_SUB):
        cqt_ref[j] = cqt[:, j * MLA_Q_SUB:(j + 1) * MLA_Q_SUB]
    ckv_ref[...] = _rms(p[:, MLA_Q_RANK:MLA_Q_RANK + MLA_KV_RANK], gkv_ref[...])
    kr_ref[...] = p[:, MLA_Q_RANK + MLA_KV_RANK:]


def _mla_proj(st, x, mods, gn, w_in_ext, g_q, g_kv):
    n = st.n
    return pl.pallas_call(
        _mla_proj_kernel,
        grid=(st.tiles,),
        in_specs=[st.tok(D_MODEL), st.mod(), _resident((2, D_MODEL)), _resident(w_in_ext.shape),
                  _resident((1, MLA_Q_RANK)), _resident((1, MLA_KV_RANK))],
        out_specs=[pl.BlockSpec((TOKEN_TILE // MLA_Q_SUB, MLA_Q_RANK, MLA_Q_SUB), lambda i: (i, 0, 0)),
                   st.tok(MLA_KV_RANK), st.tok(2 * MLA_ROPE)],
        out_shape=[jax.ShapeDtypeStruct((n // MLA_Q_SUB, MLA_Q_RANK, MLA_Q_SUB), BF16),
                   jax.ShapeDtypeStruct((n, MLA_KV_RANK), F32),
                   jax.ShapeDtypeStruct((n, 2 * MLA_ROPE), F32)],
        compiler_params=_params("parallel"),
        name="mla_proj",
    )(x, mods, gn, w_in_ext, g_q, g_kv)


def _mla_attn_kernel(cqt_ref, ckv_ref, kr_ref, tabk_ref, tabqt_ref, wuqt_ref, wukv_ref, o_ref,
                     k_scr, vt_scr, *, heads):
    @pl.when(pl.program_id(2) == 0)
    def _():
        y = kr_ref[...] * tabk_ref[...]
        kr = (y + pltpu.roll(y, MLA_ROPE, axis=1)).astype(BF16)
        ckv = ckv_ref[...]
        for h in range(heads):
            kv = _dot(ckv, wukv_ref[h])
            k_scr[h, :, 0:MLA_NOPE] = kv[:, 0:MLA_NOPE].astype(BF16)
            k_scr[h, :, MLA_NOPE:] = kr
            vt_scr[h] = kv[:, MLA_NOPE:].T.astype(BF16)

    tk = k_scr.shape[1]
    half = (tk // 2 + 255) // 256 * 256
    parts = [(0, tk)] if tk <= 256 else [(0, half), (half, tk)]
    qscale = MLA_SCALE * 1.4426950408889634

    def scores(j, h):
        cols = slice(j * MLA_Q_SUB, (j + 1) * MLA_Q_SUB)
        qt = _dot(wuqt_ref[h], cqt_ref[:, cols])
        qqt = jnp.concatenate([qt[0:MLA_NOPE] * qscale, qt[MLA_NOPE:] * (tabqt_ref[:, cols] * qscale)],
                              axis=0).astype(BF16)
        return [_dot(k_scr[h, lo:hi, :], qqt) for lo, hi in parts]

    def finish(j, h, sts):
        m = functools.reduce(jnp.maximum, [jnp.max(s, axis=0, keepdims=True) for s in sts])
        pts = [jnp.exp2(s - m) for s in sts]
        denom = functools.reduce(jnp.add, [jnp.sum(p, axis=0, keepdims=True) for p in pts])
        ot = functools.reduce(jnp.add, [_dot(vt_scr[h, :, lo:hi], p.astype(BF16))
                                        for (lo, hi), p in zip(parts, pts)])
        o_ref[j * MLA_Q_SUB:(j + 1) * MLA_Q_SUB, h * MLA_V:(h + 1) * MLA_V] = (ot / denom).T.astype(BF16)

    work = [(j, h) for j in range(cqt_ref.shape[1] // MLA_Q_SUB) for h in range(heads)]
    pending = scores(*work[0])
    for n, (j, h) in enumerate(work):
        nxt = scores(*work[n + 1]) if n + 1 < len(work) else None
        finish(j, h, pending)
        pending = nxt


def _mla_attn(batch, tq_total, cqt, ckv_all, kr_all, tabk, tabqt, w_uqt, w_ukv, heads, q_tile):
    tk = ckv_all.shape[1]
    nq = tq_total // q_tile
    once = pl.Buffered(1)
    return pl.pallas_call(
        functools.partial(_mla_attn_kernel, heads=heads),
        grid=(batch, MLA_HEADS // heads, nq),
        in_specs=[pl.BlockSpec((None, MLA_Q_RANK, q_tile), lambda b, g, i: (b * nq + i, 0, 0)),
                  pl.BlockSpec((None, tk, MLA_KV_RANK), lambda b, g, i: (b, 0, 0), pipeline_mode=once),
                  pl.BlockSpec((None, tk, 2 * MLA_ROPE), lambda b, g, i: (b, 0, 0), pipeline_mode=once),
                  pl.BlockSpec((tk, 2 * MLA_ROPE), lambda b, g, i: (0, 0), pipeline_mode=once),
                  pl.BlockSpec((2 * MLA_ROPE, q_tile), lambda b, g, i: (0, i)),
                  pl.BlockSpec((heads, 256, MLA_Q_RANK), lambda b, g, i: (g, 0, 0)),
                  pl.BlockSpec((heads, MLA_KV_RANK, 256), lambda b, g, i: (g, 0, 0))],
        out_specs=pl.BlockSpec((q_tile, heads * MLA_V), lambda b, g, i: (b * nq + i, g)),
        out_shape=jax.ShapeDtypeStruct((batch * tq_total, MLA_HEADS * MLA_V), BF16),
        scratch_shapes=[pltpu.VMEM((heads, tk, 256), BF16), pltpu.VMEM((heads, MLA_V, tk), BF16)],
        compiler_params=_params("parallel", "parallel", "arbitrary"),
        name="mla_attn",
    )(cqt, ckv_all, kr_all, tabk, tabqt, w_uqt, w_ukv)


def _mla_attn_long_kernel(cqt_ref, ckv_ref, kr_ref, tabk_ref, tabqt_ref, wuqt_ref, wukv_ref, o_ref,
                          k_scr, vt_scr, st_scr):
    n_sub = cqt_ref.shape[0]
    tk = k_scr.shape[0]
    half = (tk // 2 + 255) // 256 * 256
    parts = [(0, half), (half, tk)]
    qscale = MLA_SCALE * 1.4426950408889634

    y = kr_ref[...] * tabk_ref[...]
    kv = _dot(ckv_ref[...], wukv_ref[...])
    k_scr[:, 0:MLA_NOPE] = kv[:, 0:MLA_NOPE].astype(BF16)
    k_scr[:, MLA_NOPE:] = (y + pltpu.roll(y, MLA_ROPE, axis=1)).astype(BF16)
    vt_scr[...] = kv[:, MLA_NOPE:].T.astype(BF16)

    def scores(j, slot):
        qt = _dot(wuqt_ref[...], cqt_ref[j])
        qqt = jnp.concatenate([qt[0:MLA_NOPE] * qscale, qt[MLA_NOPE:] * (tabqt_ref[j] * qscale)],
                              axis=0).astype(BF16)
        for lo, hi in parts:
            st_scr[slot, lo:hi, :] = _dot(k_scr[lo:hi, :], qqt)

    def finish(j, slot):
        m = functools.reduce(jnp.maximum, [jnp.max(st_scr[slot, lo:hi, :], axis=0, keepdims=True)
                                           for lo, hi in parts])
        pts = [jnp.exp2(st_scr[slot, lo:hi, :] - m) for lo, hi in parts]
        denom = functools.reduce(jnp.add, [jnp.sum(p, axis=0, keepdims=True) for p in pts])
        ot = functools.reduce(jnp.add, [_dot(vt_scr[:, lo:hi], p.astype(BF16))
                                        for (lo, hi), p in zip(parts, pts)])
        rows = pl.ds(pl.multiple_of(j * MLA_Q_SUB, MLA_Q_SUB), MLA_Q_SUB)
        o_ref[rows, :] = (ot / denom).T.astype(BF16)

    scores(0, 0)

    def pair(t, carry):
        j = 2 * t
        scores(j + 1, 1)
        finish(j, 0)
        scores(j + 2, 0)
        finish(j + 1, 1)
        return carry

    lax.fori_loop(0, n_sub // 2 - 1, pair, 0)
    scores(n_sub - 1, 1)
    finish(n_sub - 2, 0)
    finish(n_sub - 1, 1)


def _mla_attn_long(batch, tq_total, cqt3, ckv_all, kr_all, tabk, tabqt3, w_uqt, w_ukv):
    tk = ckv_all.shape[1]
    n_sub = tq_total // MLA_Q_SUB
    once = pl.Buffered(1)
    return pl.pallas_call(
        _mla_attn_long_kernel,
        grid=(batch, MLA_HEADS),
        in_specs=[pl.BlockSpec((n_sub, MLA_Q_RANK, MLA_Q_SUB), lambda b, h: (b, 0, 0)),
                  pl.BlockSpec((None, tk, MLA_KV_RANK), lambda b, h: (b, 0, 0), pipeline_mode=once),
                  pl.BlockSpec((None, tk, 2 * MLA_ROPE), lambda b, h: (b, 0, 0), pipeline_mode=once),
                  pl.BlockSpec((tk, 2 * MLA_ROPE), lambda b, h: (0, 0), pipeline_mode=once),
                  pl.BlockSpec((n_sub, 2 * MLA_ROPE, MLA_Q_SUB), lambda b, h: (0, 0, 0), pipeline_mode=once),
                  pl.BlockSpec((None, 256, MLA_Q_RANK), lambda b, h: (h, 0, 0)),
                  pl.BlockSpec((None, MLA_KV_RANK, 256), lambda b, h: (h, 0, 0))],
        out_specs=pl.BlockSpec((tq_total, MLA_V), lambda b, h: (b, h)),
        out_shape=jax.ShapeDtypeStruct((batch * tq_total, MLA_HEADS * MLA_V), BF16),
        scratch_shapes=[pltpu.VMEM((tk, 256), BF16), pltpu.VMEM((MLA_V, tk), BF16),
                        pltpu.VMEM((2, tk, MLA_Q_SUB), F32)],
        compiler_params=_params("parallel", "arbitrary"),
        name="mla_attn_long",
    )(cqt3, ckv_all, kr_all, tabk, tabqt3, w_uqt, w_ukv)


def _mla_post_kernel(a_ref, x_ref, mod_ref, gn_ref, w_ref, out_ref):
    out_ref[...] = _residual(x_ref[...], _dot(a_ref[...], w_ref[...]), mod_ref, gn_ref, 1, 1.0)


def _mla_post(st, a, x, mods, gn, w_out):
    return pl.pallas_call(
        _mla_post_kernel,
        grid=(st.tiles,),
        in_specs=[st.tok(MLA_HEADS * MLA_V), st.tok(D_MODEL), st.mod(), _resident((2, D_MODEL)),
                  _resident((MLA_HEADS * MLA_V, D_MODEL))],
        out_specs=st.tok(D_MODEL),
        out_shape=jax.ShapeDtypeStruct((st.n, D_MODEL), F32),
        compiler_params=_params("parallel"),
        name="mla_post",
    )(a, x, mods, gn, w_out)


_ROPE_SWAP = tuple(list(range(16, 32)) + list(range(0, 16)) + list(range(48, 64)) + list(range(32, 48)))


def _rope_table(n_tokens):
    rows = n_tokens // GRID_W
    row = jnp.repeat(jnp.arange(rows), GRID_W).astype(F32)
    col = jnp.tile(jnp.arange(GRID_W), rows).astype(F32)
    n_pairs = MLA_ROPE // 4
    inv = ROPE_BASE ** (-jnp.arange(n_pairs, dtype=F32) / n_pairs)
    ar, ac = row[:, None] * inv, col[:, None] * inv
    cos = jnp.concatenate([jnp.cos(ar), jnp.cos(ar), jnp.cos(ac), jnp.cos(ac)], axis=1)
    sin = jnp.concatenate([-jnp.sin(ar), jnp.sin(ar), -jnp.sin(ac), jnp.sin(ac)], axis=1)
    return jnp.concatenate([cos, sin], axis=1)


def _identity_table(n_tokens):
    return jnp.concatenate([jnp.ones((n_tokens, MLA_ROPE), F32), jnp.zeros((n_tokens, MLA_ROPE), F32)], axis=1)


def kernel(x_prompt, x_sample, state_gla, cache_mla_ckv, cache_mla_krope, c, c_ctx, w_mod, b_mod, g_norm, w_ffn_in, w_ffn_out, gla_w_in, gla_w_gate, gla_b_gate, gla_g_out, gla_w_out, mla_w_in, mla_g_q, mla_g_kv, mla_w_uq, mla_w_ukv, mla_w_out):
    batch, seq = x_prompt.shape[0], x_prompt.shape[1]
    dec_batch, dec_seq = x_sample.shape[0], x_sample.shape[1]
    past = cache_mla_ckv.shape[2]
    ctx = _Stream(batch, seq, dec_batch, per_batch_mod=False)
    lat = _Stream(dec_batch, dec_seq, 0, per_batch_mod=True)

    cond8 = jnp.concatenate([c, c_ctx[None, :], jnp.zeros((8 - dec_batch - 1, D_MODEL), F32)], axis=0)
    mods = _modulation(cond8, w_mod, b_mod)

    xs = {ctx: x_prompt.reshape(ctx.n, D_MODEL), lat: x_sample.reshape(lat.n, D_MODEL)}
    w_ffn_in_b = w_ffn_in.astype(BF16)
    w_ffn_out_b = w_ffn_out.astype(BF16)
    gla_states, mla_ckvs, mla_krs = [], [], []

    for l in range(DEPTH):
        j = l // 2
        for st in (ctx, lat):
            xs[st] = _ffn(st, xs[st], mods[l], g_norm[l, 0], w_ffn_in_b[l, 0], w_ffn_out_b[l, 0], 0)

        if l % 2 == 0:
            w_in = gla_w_in[j].astype(BF16)
            w_qkvr, w_z = w_in[:, :2 * GLA_QK + 2 * GLA_VV], w_in[:, 2 * GLA_QK + 2 * GLA_VV:]
            zeros = jnp.zeros((GLA_GATE_RANK, GLA_QK), F32)
            w_gate_bd = jnp.concatenate([jnp.concatenate([gla_w_gate[j, 0], zeros], axis=1),
                                         jnp.concatenate([zeros, gla_w_gate[j, 1]], axis=1)], axis=0).astype(BF16)
            b_gate = gla_b_gate[j].reshape(1, 2 * GLA_QK)
            g_out = gla_g_out[j].reshape(1, GLA_VV)
            w_out = gla_w_out[j].astype(BF16)
            for st in (ctx, lat):
                q, k, v, r, la = _gla_proj(st, xs[st], mods[l], g_norm[l, 1], w_qkvr, w_z, w_gate_bd, b_gate)
                if st is ctx:
                    o, s_t = _gla_scan(st, q, k, v, la, None, True)
                    gla_states.append(jnp.swapaxes(s_t, -1, -2))
                else:
                    s0_t = jnp.swapaxes(state_gla[:, j].astype(F32), -1, -2)
                    (o,) = _gla_scan(st, q, k, v, la, s0_t, False)
                xs[st] = _gla_post(st, o, r, xs[st], mods[l], g_norm[l, 1], g_out, w_out)
        else:
            swap = jnp.array(_ROPE_SWAP)
            w_in = mla_w_in[j]
            w_in_ext = jnp.concatenate([w_in, w_in[:, MLA_Q_RANK + MLA_KV_RANK:][:, swap]], axis=1).astype(BF16)
            w_uq = mla_w_uq[j].reshape(MLA_Q_RANK, MLA_HEADS, MLA_NOPE + MLA_ROPE)
            w_uq = jnp.concatenate([w_uq, w_uq[:, :, MLA_NOPE:][:, :, swap]], axis=2)
            w_uqt = jnp.transpose(w_uq, (1, 2, 0)).astype(BF16)
            w_ukv = jnp.transpose(mla_w_ukv[j].reshape(MLA_KV_RANK, MLA_HEADS, MLA_NOPE + MLA_V), (1, 0, 2)).astype(BF16)
            g_q = mla_g_q[j].reshape(1, MLA_Q_RANK)
            g_kv = mla_g_kv[j].reshape(1, MLA_KV_RANK)
            w_out = mla_w_out[j].astype(BF16)
            for st in (ctx, lat):
                cqt, ckv, kr2 = _mla_proj(st, xs[st], mods[l], g_norm[l, 1], w_in_ext, g_q, g_kv)
                ckv3 = ckv.reshape(st.batch, st.seq, MLA_KV_RANK)
                kr3 = kr2.reshape(st.batch, st.seq, 2 * MLA_ROPE)
                if st is ctx:
                    mla_ckvs.append(ckv3)
                    mla_krs.append(kr3[:, :, :MLA_ROPE])
                    tab = _identity_table(st.seq)
                    a = _mla_attn(st.batch, st.seq, cqt, ckv3.astype(BF16), kr3, tab, tab.T, w_uqt, w_ukv,
                                  heads=MLA_HEADS, q_tile=st.seq)
                else:
                    ckv_all = jnp.concatenate([cache_mla_ckv[:, j].astype(F32), ckv3], axis=1).astype(BF16)
                    kr_ctx = cache_mla_krope[:, j].astype(F32)
                    kr_all = jnp.concatenate([jnp.concatenate([kr_ctx, jnp.zeros_like(kr_ctx)], axis=2), kr3], axis=1)
                    tabq = _rope_table(st.seq)
                    tabk = jnp.concatenate([_identity_table(past), tabq], axis=0)
                    tabqt3 = jnp.transpose(tabq.reshape(st.seq // MLA_Q_SUB, MLA_Q_SUB, 2 * MLA_ROPE), (0, 2, 1))
                    a = _mla_attn_long(st.batch, st.seq, cqt, ckv_all, kr_all, tabk, tabqt3, w_uqt, w_ukv)
                xs[st] = _mla_post(st, a, xs[st], mods[l], g_norm[l, 1], w_out)

        for st in (ctx, lat):
            xs[st] = _ffn(st, xs[st], mods[l], g_norm[l, 2], w_ffn_in_b[l, 1], w_ffn_out_b[l, 1], 2)

    yp = xs[ctx].reshape(batch, seq, D_MODEL)
    ys = xs[lat].reshape(dec_batch, dec_seq, D_MODEL)
    new_state_gla = jnp.stack(gla_states, axis=1).astype(x_prompt.dtype)
    new_cache_mla_ckv = jnp.stack(mla_ckvs, axis=1)
    new_cache_mla_krope = jnp.stack(mla_krs, axis=1)
    return (yp, ys, new_state_gla, new_cache_mla_ckv, new_cache_mla_krope)
```

```python
import functools

import jax
import jax.numpy as jnp
from jax import lax
from jax.experimental import pallas as pl
from jax.experimental.pallas import tpu as pltpu

F32 = jnp.float32
BF16 = jnp.bfloat16

D_MODEL = 1024
DEPTH = 2
GRID_W = 64
D_FF = 2816
MACARON_W = 0.5
N_MOD = 9
EPS = 1e-6

GLA_HEADS = 4
GLA_DK = 128
GLA_DV = 256
GLA_GATE_RANK = 16
GLA_TAU = 16.0
GLA_QK = GLA_HEADS * GLA_DK
GLA_VV = GLA_HEADS * GLA_DV

MLA_HEADS = 16
MLA_NOPE = 128
MLA_ROPE = 64
MLA_V = 128
MLA_Q_RANK = 512
MLA_KV_RANK = 256
MLA_SCALE = (MLA_NOPE + MLA_ROPE) ** -0.5
ROPE_BASE = 10000.0

VMEM_LIMIT_BYTES = 56 * 1024 * 1024

TOKEN_TILE = 512
FFN_CHUNK = 1408
GLA_CHUNK = 64
GLA_BLOCK = 512
MLA_Q_SUB = 256


def _params(*sem):
    return pltpu.CompilerParams(dimension_semantics=sem, vmem_limit_bytes=VMEM_LIMIT_BYTES)


def _resident(shape):
    nd = len(shape)
    return pl.BlockSpec(shape, lambda *_: (0,) * nd, pipeline_mode=pl.Buffered(1))


def _silu(x):
    return x * (1.0 / (1.0 + jnp.exp(-x)))


def _rms(x, g):
    return x * lax.rsqrt(jnp.mean(x * x, axis=-1, keepdims=True) + EPS) * g


def _pre(x, mod_ref, gn_ref, s):
    return _rms(x, gn_ref[0:1, :]) * (1.0 + mod_ref[3 * s + 1]) + mod_ref[3 * s]


def _residual(x, y, mod_ref, gn_ref, s, weight):
    return x + (weight * mod_ref[3 * s + 2]) * _rms(y, gn_ref[1:2, :])


def _dot(a, b):
    return jnp.dot(a, b, preferred_element_type=F32)


def _dot_nt(a, b):
    return lax.dot_general(a, b, (((1,), (1,)), ((), ())), preferred_element_type=F32)


def _dot_tn(a, b):
    return lax.dot_general(a, b, (((0,), (0,)), ((), ())), preferred_element_type=F32)


def _mod_kernel(cond_ref, w_ref, b_ref, o_ref):
    a = _silu(cond_ref[...]).astype(BF16)
    o_ref[...] = _dot(a, w_ref[...].astype(BF16)) + b_ref[...]


def _modulation(cond8, w_mod, b_mod):
    out = pl.pallas_call(
        _mod_kernel,
        grid=(DEPTH, N_MOD),
        in_specs=[
            pl.BlockSpec((8, D_MODEL), lambda l, j: (0, 0)),
            pl.BlockSpec((None, D_MODEL, D_MODEL), lambda l, j: (l, 0, j)),
            pl.BlockSpec((None, None, 1, D_MODEL), lambda l, j: (l, j, 0, 0)),
        ],
        out_specs=pl.BlockSpec((None, None, 8, D_MODEL), lambda l, j: (l, j, 0, 0)),
        out_shape=jax.ShapeDtypeStruct((DEPTH, N_MOD, 8, D_MODEL), F32),
        compiler_params=_params("parallel", "parallel"),
        name="modulation",
    )(cond8, w_mod, b_mod.reshape(DEPTH, N_MOD, 1, D_MODEL))
    return out.reshape(DEPTH, N_MOD, 8, 1, D_MODEL)


class _Stream:
    def __init__(self, batch, seq, mod_row0, per_batch_mod):
        self.batch, self.seq = batch, seq
        self.n = batch * seq
        self.tiles = self.n // TOKEN_TILE
        tiles_per_batch = seq // TOKEN_TILE
        if per_batch_mod:
            self.mod_row = lambda i: mod_row0 + i // tiles_per_batch
        else:
            self.mod_row = lambda i: mod_row0

    def tok(self, width):
        return pl.BlockSpec((TOKEN_TILE, width), lambda i: (i, 0))

    def mod(self):
        return pl.BlockSpec((N_MOD, None, 1, D_MODEL), lambda i: (0, self.mod_row(i), 0, 0))


def _ffn_kernel(x_ref, mod_ref, gn_ref, win_ref, wout_ref, o_ref, *, s):
    x = x_ref[...]
    h = _pre(x, mod_ref, gn_ref, s).astype(BF16)
    y = jnp.zeros((TOKEN_TILE, D_MODEL), F32)
    for lo in range(0, D_FF, FFN_CHUNK):
        g = _dot(h, win_ref[:, lo:lo + FFN_CHUNK])
        u = _dot(h, win_ref[:, D_FF + lo:D_FF + lo + FFN_CHUNK])
        a = (_silu(g) * u).astype(BF16)
        y = y + _dot(a, wout_ref[lo:lo + FFN_CHUNK, :])
    o_ref[...] = _residual(x, y, mod_ref, gn_ref, s, MACARON_W)


def _ffn(st, x, mods, gn, w_in, w_out, s):
    return pl.pallas_call(
        functools.partial(_ffn_kernel, s=s),
        grid=(st.tiles,),
        in_specs=[st.tok(D_MODEL), st.mod(), _resident((2, D_MODEL)),
                  _resident((D_MODEL, 2 * D_FF)), _resident((D_FF, D_MODEL))],
        out_specs=st.tok(D_MODEL),
        out_shape=jax.ShapeDtypeStruct((st.n, D_MODEL), F32),
        compiler_params=_params("parallel"),
        name="ffn",
    )(x, mods, gn, w_in, w_out)


def _gla_proj_kernel(x_ref, mod_ref, gn_ref, w_ref, wz_ref, wg_ref, bg_ref,
                     q_ref, k_ref, v_ref, r_ref, la_ref):
    h = _pre(x_ref[...], mod_ref, gn_ref, 1).astype(BF16)
    q_ref[...] = _dot(h, w_ref[:, 0:GLA_QK]) * (GLA_DK ** -0.5)
    k_ref[...] = _dot(h, w_ref[:, GLA_QK:2 * GLA_QK])
    v_ref[...] = _dot(h, w_ref[:, 2 * GLA_QK:2 * GLA_QK + GLA_VV]).astype(BF16)
    r_ref[...] = _dot(h, w_ref[:, 2 * GLA_QK + GLA_VV:2 * GLA_QK + 2 * GLA_VV]).astype(BF16)
    z = _dot(h, wz_ref[...]).astype(BF16)
    logit = _dot(z, wg_ref[...]) + bg_ref[...]
    la_ref[...] = (jnp.minimum(logit, 0.0) - jnp.log(1.0 + jnp.exp(-jnp.abs(logit)))) * (1.0 / GLA_TAU)


def _gla_proj(st, x, mods, gn, w_qkvr, w_z, w_gate_bd, b_gate):
    n = st.n
    return pl.pallas_call(
        _gla_proj_kernel,
        grid=(st.tiles,),
        in_specs=[st.tok(D_MODEL), st.mod(), _resident((2, D_MODEL)),
                  _resident(w_qkvr.shape), _resident(w_z.shape),
                  _resident(w_gate_bd.shape), _resident(b_gate.shape)],
        out_specs=[st.tok(GLA_QK), st.tok(GLA_QK), st.tok(GLA_VV), st.tok(GLA_VV), st.tok(2 * GLA_QK)],
        out_shape=[jax.ShapeDtypeStruct((n, GLA_QK), F32), jax.ShapeDtypeStruct((n, GLA_QK), F32),
                   jax.ShapeDtypeStruct((n, GLA_VV), BF16), jax.ShapeDtypeStruct((n, GLA_VV), BF16),
                   jax.ShapeDtypeStruct((n, 2 * GLA_QK), F32)],
        compiler_params=_params("parallel"),
        name="gla_proj",
    )(x, mods, gn, w_qkvr, w_z, w_gate_bd, b_gate)


def _gla_scan_kernel(*refs, has_state, want_state):
    it = iter(refs)
    q_ref, k_ref, v_ref, la_ref = next(it), next(it), next(it), next(it)
    s0_ref = next(it) if has_state else None
    o_ref = next(it)
    sout_ref = next(it) if want_state else None
    st_ref = next(it)

    d = pl.program_id(1)
    blk = pl.program_id(2)
    n_chunks = q_ref.shape[0] // GLA_CHUNK
    head_k = [slice(h * GLA_DK, (h + 1) * GLA_DK) for h in range(GLA_HEADS)]
    head_v = [slice(h * GLA_DV, (h + 1) * GLA_DV) for h in range(GLA_HEADS)]

    @pl.when(blk == 0)
    def _():
        if has_state:
            st_ref[...] = s0_ref[...]
        else:
            st_ref[...] = jnp.zeros(st_ref.shape, F32)

    ii = lax.broadcasted_iota(jnp.int32, (GLA_CHUNK, GLA_CHUNK), 0)
    jj = lax.broadcasted_iota(jnp.int32, (GLA_CHUNK, GLA_CHUNK), 1)

    row = lax.broadcasted_iota(jnp.int32, (GLA_CHUNK, GLA_QK), 0)

    def cumsum(x, backward):
        step = 1
        while step < GLA_CHUNK:
            if backward:
                x = x + jnp.where(row < GLA_CHUNK - step, pltpu.roll(x, GLA_CHUNK - step, axis=0), 0.0)
            else:
                x = x + jnp.where(row >= step, pltpu.roll(x, step, axis=0), 0.0)
            step *= 2
        return x

    def run(backward):
        mask = (jj >= ii) if backward else (jj <= ii)
        order = list(range(n_chunks - 1, -1, -1) if backward else range(n_chunks))

        def prep(c):
            rows = slice(c * GLA_CHUNK, (c + 1) * GLA_CHUNK)
            la = la_ref[rows, :]
            b = cumsum(la, backward)
            btot = jnp.sum(la, axis=0, keepdims=True)
            bm = b[GLA_CHUNK // 2:GLA_CHUNK // 2 + 1, :]
            q = q_ref[rows, :]
            k = k_ref[rows, :]
            qm = (q * jnp.exp(b - bm)).astype(BF16)
            km = (k * jnp.exp(bm - b)).astype(BF16)
            att = [jnp.where(mask, _dot_nt(qm[:, sk], km[:, sk]), 0.0).astype(BF16) for sk in head_k]
            return dict(rows=rows, att=att, v=v_ref[rows, :], qb=(q * jnp.exp(b)).astype(BF16),
                        kt=(k * jnp.exp(btot - b)).astype(BF16), dec=jnp.exp(btot))

        def update(p):
            for h, (sk, sv) in enumerate(zip(head_k, head_v)):
                s = st_ref[h]
                v = p["v"][:, sv]
                o_ref[p["rows"], sv] = (_dot(p["att"][h], v) + _dot(p["qb"][:, sk], s.astype(BF16))).astype(BF16)
                dec_t = jnp.broadcast_to(p["dec"][:, sk], (GLA_DK, GLA_DK)).T
                st_ref[h] = s * jnp.concatenate([dec_t] * (GLA_DV // GLA_DK), axis=1) + _dot_tn(p["kt"][:, sk], v)

        nxt = prep(order[0])
        for n in range(n_chunks):
            cur, nxt = nxt, (prep(order[n + 1]) if n + 1 < n_chunks else None)
            update(cur)

    @pl.when(d == 0)
    def _():
        run(False)

    @pl.when(d == 1)
    def _():
        run(True)

    if want_state:
        @pl.when(blk == pl.num_programs(2) - 1)
        def _():
            sout_ref[...] = st_ref[...]


def _gla_scan(st, q, k, v, la, s0, want_state):
    block = min(GLA_BLOCK, st.seq)
    nb = st.seq // block
    has_state = s0 is not None

    def row(b, d, i):
        return b * nb + i + d * (nb - 1 - 2 * i)

    state_spec = pl.BlockSpec((None, None, GLA_HEADS, GLA_DK, GLA_DV), lambda b, d, i: (b, d, 0, 0, 0))
    in_specs = [pl.BlockSpec((block, GLA_QK), lambda b, d, i: (row(b, d, i), 0)),
                pl.BlockSpec((block, GLA_QK), lambda b, d, i: (row(b, d, i), 0)),
                pl.BlockSpec((block, GLA_VV), lambda b, d, i: (row(b, d, i), 0)),
                pl.BlockSpec((block, GLA_QK), lambda b, d, i: (row(b, d, i), d))]
    args = [q, k, v, la]
    if has_state:
        in_specs.append(state_spec)
        args.append(s0)
    out_specs = [pl.BlockSpec((None, block, GLA_VV), lambda b, d, i: (d, row(b, d, i), 0))]
    out_shape = [jax.ShapeDtypeStruct((2, st.n, GLA_VV), BF16)]
    if want_state:
        out_specs.append(state_spec)
        out_shape.append(jax.ShapeDtypeStruct((st.batch, 2, GLA_HEADS, GLA_DK, GLA_DV), F32))
    return pl.pallas_call(
        functools.partial(_gla_scan_kernel, has_state=has_state, want_state=want_state),
        grid=(st.batch, 2, nb),
        in_specs=in_specs,
        out_specs=out_specs,
        out_shape=out_shape,
        scratch_shapes=[pltpu.VMEM((GLA_HEADS, GLA_DK, GLA_DV), F32)],
        compiler_params=_params("parallel", "parallel", "arbitrary"),
        name="gla_scan",
    )(*args)


def _gla_post_kernel(o_ref, r_ref, x_ref, mod_ref, gn_ref, go_ref, w_ref, out_ref):
    o = o_ref[0].astype(F32) + o_ref[1].astype(F32)
    heads = []
    for h in range(GLA_HEADS):
        sv = slice(h * GLA_DV, (h + 1) * GLA_DV)
        heads.append(_rms(o[:, sv], go_ref[:, sv]))
    g = (jnp.concatenate(heads, axis=1) * _silu(r_ref[...].astype(F32))).astype(BF16)
    out_ref[...] = _residual(x_ref[...], _dot(g, w_ref[...]), mod_ref, gn_ref, 1, 1.0)


def _gla_post(st, o, r, x, mods, gn, g_out, w_out):
    return pl.pallas_call(
        _gla_post_kernel,
        grid=(st.tiles,),
        in_specs=[pl.BlockSpec((2, TOKEN_TILE, GLA_VV), lambda i: (0, i, 0)),
                  st.tok(GLA_VV), st.tok(D_MODEL), st.mod(), _resident((2, D_MODEL)),
                  _resident((1, GLA_VV)), _resident((GLA_VV, D_MODEL))],
        out_specs=st.tok(D_MODEL),
        out_shape=jax.ShapeDtypeStruct((st.n, D_MODEL), F32),
        compiler_params=_params("parallel"),
        name="gla_post",
    )(o, r, x, mods, gn, g_out, w_out)


MLA_Q_SCALE = MLA_SCALE * 1.4426950408889634


def _mla_proj_kernel(x_ref, mod_ref, gn_ref, w_ref, gq_ref, gkv_ref, wuqt_ref, tabqt_ref,
                     qt_ref, ckv_ref, kr_ref):
    h = _pre(x_ref[...], mod_ref, gn_ref, 1).astype(BF16)
    p = _dot(h, w_ref[...])
    ckv_ref[...] = _rms(p[:, MLA_Q_RANK:MLA_Q_RANK + MLA_KV_RANK], gkv_ref[...])
    kr_ref[...] = p[:, MLA_Q_RANK + MLA_KV_RANK:]
    cqt = _rms(p[:, 0:MLA_Q_RANK], gq_ref[...]).T.astype(BF16)
    tab = tabqt_ref[...] * MLA_Q_SCALE
    for hd in range(MLA_HEADS):
        qt = _dot(wuqt_ref[hd], cqt)
        for j in range(TOKEN_TILE // MLA_Q_SUB):
            cols = slice(j * MLA_Q_SUB, (j + 1) * MLA_Q_SUB)
            qt_ref[j, hd] = jnp.concatenate([qt[0:MLA_NOPE, cols] * MLA_Q_SCALE, qt[MLA_NOPE:, cols] * tab[:, cols]],
                                            axis=0).astype(BF16)


def _mla_proj(st, x, mods, gn, w_in_ext, g_q, g_kv, w_uqt, tabqt):
    n = st.n
    n_sub = TOKEN_TILE // MLA_Q_SUB
    tab_tiles = tabqt.shape[1] // TOKEN_TILE
    return pl.pallas_call(
        _mla_proj_kernel,
        grid=(st.tiles,),
        in_specs=[st.tok(D_MODEL), st.mod(), _resident((2, D_MODEL)), _resident(w_in_ext.shape),
                  _resident((1, MLA_Q_RANK)), _resident((1, MLA_KV_RANK)), _resident(w_uqt.shape),
                  pl.BlockSpec((2 * MLA_ROPE, TOKEN_TILE), lambda i: (0, i % tab_tiles))],
        out_specs=[pl.BlockSpec((n_sub, MLA_HEADS, 256, MLA_Q_SUB), lambda i: (i, 0, 0, 0)),
                   st.tok(MLA_KV_RANK), st.tok(2 * MLA_ROPE)],
        out_shape=[jax.ShapeDtypeStruct((n // MLA_Q_SUB, MLA_HEADS, 256, MLA_Q_SUB), BF16),
                   jax.ShapeDtypeStruct((n, MLA_KV_RANK), F32),
                   jax.ShapeDtypeStruct((n, 2 * MLA_ROPE), F32)],
        compiler_params=_params("parallel"),
        name="mla_proj",
    )(x, mods, gn, w_in_ext, g_q, g_kv, w_uqt, tabqt)


def _mla_build_kv(ckv, kr_dup, wukv, k_dst, vt_dst):
    kv = _dot(ckv, wukv)
    k_dst[:, 0:MLA_NOPE] = kv[:, 0:MLA_NOPE].astype(BF16)
    k_dst[:, MLA_NOPE:] = kr_dup
    vt_dst[...] = kv[:, MLA_NOPE:].T.astype(BF16)


def _mla_key_parts(tk):
    half = (tk // 2 + 255) // 256 * 256
    return [(0, tk)] if tk <= 256 else [(0, half), (half, tk)]


def _mla_softmax_pv(sts, vt_parts):
    m = functools.reduce(jnp.maximum, [jnp.max(s, axis=0, keepdims=True) for s in sts])
    pts = [jnp.exp2(s - m) for s in sts]
    denom = functools.reduce(jnp.add, [jnp.sum(p, axis=0, keepdims=True) for p in pts])
    ot = functools.reduce(jnp.add, [_dot(vt, p.astype(BF16)) for vt, p in zip(vt_parts, pts)])
    return (ot / denom).T.astype(BF16)


def _mla_attn_kernel(qt_ref, ckv_ref, kr_ref, wukv_ref, o_ref, k_scr, vt_scr):
    kr = kr_ref[...]
    kr_dup = (kr + pltpu.roll(kr, MLA_ROPE, axis=1)).astype(BF16)
    ckv = ckv_ref[...]

    def scores(h):
        _mla_build_kv(ckv, kr_dup, wukv_ref[h], k_scr.at[h], vt_scr.at[h])
        return _dot(k_scr[h], qt_ref[h])

    nxt = scores(0)
    for h in range(MLA_HEADS):
        cur, nxt = nxt, (scores(h + 1) if h + 1 < MLA_HEADS else None)
        o_ref[:, h * MLA_V:(h + 1) * MLA_V] = _mla_softmax_pv([cur], [vt_scr[h]])


def _mla_attn(batch, seq, qt, ckv3, kr_masked, w_ukv):
    return pl.pallas_call(
        _mla_attn_kernel,
        grid=(batch,),
        in_specs=[pl.BlockSpec((None, MLA_HEADS, 256, seq), lambda b: (b, 0, 0, 0)),
                  pl.BlockSpec((None, seq, MLA_KV_RANK), lambda b: (b, 0, 0)),
                  pl.BlockSpec((None, seq, 2 * MLA_ROPE), lambda b: (b, 0, 0)),
                  _resident(w_ukv.shape)],
        out_specs=pl.BlockSpec((seq, MLA_HEADS * MLA_V), lambda b: (b, 0)),
        out_shape=jax.ShapeDtypeStruct((batch * seq, MLA_HEADS * MLA_V), BF16),
        scratch_shapes=[pltpu.VMEM((MLA_HEADS, seq, 256), BF16), pltpu.VMEM((MLA_HEADS, MLA_V, seq), BF16)],
        compiler_params=_params("parallel"),
        name="mla_attn",
    )(qt, ckv3, kr_masked, w_ukv)


def _mla_attn_long_kernel(qt_ref, ckv_ref, kr_ref, tabk_ref, wukv_ref, o_ref, k_scr, vt_scr, st_scr):
    n_sub = qt_ref.shape[0]
    parts = _mla_key_parts(k_scr.shape[0])

    y = kr_ref[...] * tabk_ref[...]
    _mla_build_kv(ckv_ref[...], (y + pltpu.roll(y, MLA_ROPE, axis=1)).astype(BF16), wukv_ref[...], k_scr, vt_scr)

    def scores(j, slot):
        qt = qt_ref[j]
        for lo, hi in parts:
            st_scr[slot, lo:hi, :] = _dot(k_scr[lo:hi, :], qt)

    def finish(j, slot):
        rows = pl.ds(pl.multiple_of(j * MLA_Q_SUB, MLA_Q_SUB), MLA_Q_SUB)
        o_ref[rows, :] = _mla_softmax_pv([st_scr[slot, lo:hi, :] for lo, hi in parts],
                                         [vt_scr[:, lo:hi] for lo, hi in parts])

    scores(0, 0)

    def pair(t, carry):
        j = 2 * t
        scores(j + 1, 1)
        finish(j, 0)
        scores(jnp.minimum(j + 2, n_sub - 1), 0)
        finish(j + 1, 1)
        return carry

    lax.fori_loop(0, n_sub // 2, pair, 0)


def _mla_attn_long(batch, tq_total, qt, ckv_all, kr_all, tabk, w_ukv):
    tk = ckv_all.shape[1]
    n_sub = tq_total // MLA_Q_SUB
    once = pl.Buffered(1)
    return pl.pallas_call(
        _mla_attn_long_kernel,
        grid=(batch, MLA_HEADS),
        in_specs=[pl.BlockSpec((n_sub, None, 256, MLA_Q_SUB), lambda b, h: (b, h, 0, 0)),
                  pl.BlockSpec((None, tk, MLA_KV_RANK), lambda b, h: (b, 0, 0), pipeline_mode=once),
                  pl.BlockSpec((None, tk, 2 * MLA_ROPE), lambda b, h: (b, 0, 0), pipeline_mode=once),
                  pl.BlockSpec((tk, 2 * MLA_ROPE), lambda b, h: (0, 0), pipeline_mode=once),
                  pl.BlockSpec((None, MLA_KV_RANK, 256), lambda b, h: (h, 0, 0))],
        out_specs=pl.BlockSpec((tq_total, MLA_V), lambda b, h: (b, h)),
        out_shape=jax.ShapeDtypeStruct((batch * tq_total, MLA_HEADS * MLA_V), BF16),
        scratch_shapes=[pltpu.VMEM((tk, 256), BF16), pltpu.VMEM((MLA_V, tk), BF16),
                        pltpu.VMEM((2, tk, MLA_Q_SUB), F32)],
        compiler_params=_params("parallel", "arbitrary"),
        name="mla_attn_long",
    )(qt, ckv_all, kr_all, tabk, w_ukv)


def _mla_post_kernel(a_ref, x_ref, mod_ref, gn_ref, w_ref, out_ref):
    out_ref[...] = _residual(x_ref[...], _dot(a_ref[...], w_ref[...]), mod_ref, gn_ref, 1, 1.0)


def _mla_post(st, a, x, mods, gn, w_out):
    return pl.pallas_call(
        _mla_post_kernel,
        grid=(st.tiles,),
        in_specs=[st.tok(MLA_HEADS * MLA_V), st.tok(D_MODEL), st.mod(), _resident((2, D_MODEL)),
                  _resident((MLA_HEADS * MLA_V, D_MODEL))],
        out_specs=st.tok(D_MODEL),
        out_shape=jax.ShapeDtypeStruct((st.n, D_MODEL), F32),
        compiler_params=_params("parallel"),
        name="mla_post",
    )(a, x, mods, gn, w_out)


_ROPE_SWAP = tuple(list(range(16, 32)) + list(range(0, 16)) + list(range(48, 64)) + list(range(32, 48)))


def _rope_table(n_tokens):
    rows = n_tokens // GRID_W
    row = jnp.repeat(jnp.arange(rows), GRID_W).astype(F32)
    col = jnp.tile(jnp.arange(GRID_W), rows).astype(F32)
    n_pairs = MLA_ROPE // 4
    inv = ROPE_BASE ** (-jnp.arange(n_pairs, dtype=F32) / n_pairs)
    ar, ac = row[:, None] * inv, col[:, None] * inv
    cos = jnp.concatenate([jnp.cos(ar), jnp.cos(ar), jnp.cos(ac), jnp.cos(ac)], axis=1)
    sin = jnp.concatenate([-jnp.sin(ar), jnp.sin(ar), -jnp.sin(ac), jnp.sin(ac)], axis=1)
    return jnp.concatenate([cos, sin], axis=1)


def _identity_table(n_tokens):
    return jnp.concatenate([jnp.ones((n_tokens, MLA_ROPE), F32), jnp.zeros((n_tokens, MLA_ROPE), F32)], axis=1)


def kernel(x_prompt, x_sample, state_gla, cache_mla_ckv, cache_mla_krope, c, c_ctx, w_mod, b_mod, g_norm, w_ffn_in, w_ffn_out, gla_w_in, gla_w_gate, gla_b_gate, gla_g_out, gla_w_out, mla_w_in, mla_g_q, mla_g_kv, mla_w_uq, mla_w_ukv, mla_w_out):
    batch, seq = x_prompt.shape[0], x_prompt.shape[1]
    dec_batch, dec_seq = x_sample.shape[0], x_sample.shape[1]
    past = cache_mla_ckv.shape[2]
    ctx = _Stream(batch, seq, dec_batch, per_batch_mod=False)
    lat = _Stream(dec_batch, dec_seq, 0, per_batch_mod=True)

    cond8 = jnp.concatenate([c, c_ctx[None, :], jnp.zeros((8 - dec_batch - 1, D_MODEL), F32)], axis=0)
    mods = _modulation(cond8, w_mod, b_mod)

    xs = {ctx: x_prompt.reshape(ctx.n, D_MODEL), lat: x_sample.reshape(lat.n, D_MODEL)}
    gla_states, mla_ckvs, mla_krs = [], [], []

    for l in range(DEPTH):
        j = l // 2
        w_ffn = [(w_ffn_in[l, s].astype(BF16), w_ffn_out[l, s].astype(BF16)) for s in range(2)]
        for st in (ctx, lat):
            xs[st] = _ffn(st, xs[st], mods[l], g_norm[l, 0], w_ffn[0][0], w_ffn[0][1], 0)

        if l % 2 == 0:
            w_in = gla_w_in[j].astype(BF16)
            w_qkvr, w_z = w_in[:, :2 * GLA_QK + 2 * GLA_VV], w_in[:, 2 * GLA_QK + 2 * GLA_VV:]
            zeros = jnp.zeros((GLA_GATE_RANK, GLA_QK), F32)
            w_gate_bd = jnp.concatenate([jnp.concatenate([gla_w_gate[j, 0], zeros], axis=1),
                                         jnp.concatenate([zeros, gla_w_gate[j, 1]], axis=1)], axis=0).astype(BF16)
            b_gate = gla_b_gate[j].reshape(1, 2 * GLA_QK)
            g_out = gla_g_out[j].reshape(1, GLA_VV)
            w_out = gla_w_out[j].astype(BF16)
            for st in (ctx, lat):
                q, k, v, r, la = _gla_proj(st, xs[st], mods[l], g_norm[l, 1], w_qkvr, w_z, w_gate_bd, b_gate)
                if st is ctx:
                    o, s_fin = _gla_scan(st, q, k, v, la, None, True)
                    gla_states.append(s_fin)
                else:
                    (o,) = _gla_scan(st, q, k, v, la, state_gla[:, j].astype(F32), False)
                xs[st] = _gla_post(st, o, r, xs[st], mods[l], g_norm[l, 1], g_out, w_out)
        else:
            swap = jnp.array(_ROPE_SWAP)
            w_in = mla_w_in[j]
            w_in_ext = jnp.concatenate([w_in, w_in[:, MLA_Q_RANK + MLA_KV_RANK:][:, swap]], axis=1).astype(BF16)
            w_uq = mla_w_uq[j].reshape(MLA_Q_RANK, MLA_HEADS, MLA_NOPE + MLA_ROPE)
            w_uq = jnp.concatenate([w_uq, w_uq[:, :, MLA_NOPE:][:, :, swap]], axis=2)
            w_uqt = jnp.transpose(w_uq, (1, 2, 0)).astype(BF16)
            w_ukv = jnp.transpose(mla_w_ukv[j].reshape(MLA_KV_RANK, MLA_HEADS, MLA_NOPE + MLA_V), (1, 0, 2)).astype(BF16)
            g_q = mla_g_q[j].reshape(1, MLA_Q_RANK)
            g_kv = mla_g_kv[j].reshape(1, MLA_KV_RANK)
            w_out = mla_w_out[j].astype(BF16)
            tabq = _rope_table(lat.seq)
            tabqt = {ctx: _identity_table(TOKEN_TILE).T, lat: tabq.T}
            for st in (ctx, lat):
                qt, ckv, kr2 = _mla_proj(st, xs[st], mods[l], g_norm[l, 1], w_in_ext, g_q, g_kv, w_uqt, tabqt[st])
                ckv3 = ckv.reshape(st.batch, st.seq, MLA_KV_RANK)
                kr3 = kr2.reshape(st.batch, st.seq, 2 * MLA_ROPE)
                if st is ctx:
                    mla_ckvs.append(ckv3)
                    kr = kr3[:, :, :MLA_ROPE]
                    mla_krs.append(kr)
                    kr_masked = jnp.concatenate([kr, jnp.zeros_like(kr)], axis=2)
                    a = _mla_attn(st.batch, st.seq, qt, ckv3.astype(BF16), kr_masked, w_ukv)
                else:
                    ckv_all = jnp.concatenate([cache_mla_ckv[:, j].astype(F32), ckv3], axis=1).astype(BF16)
                    kr_ctx = cache_mla_krope[:, j].astype(F32)
                    kr_all = jnp.concatenate([jnp.concatenate([kr_ctx, jnp.zeros_like(kr_ctx)], axis=2), kr3], axis=1)
                    tabk = jnp.concatenate([_identity_table(past), tabq], axis=0)
                    a = _mla_attn_long(st.batch, st.seq, qt, ckv_all, kr_all, tabk, w_ukv)
                xs[st] = _mla_post(st, a, xs[st], mods[l], g_norm[l, 1], w_out)

        for st in (ctx, lat):
            xs[st] = _ffn(st, xs[st], mods[l], g_norm[l, 2], w_ffn[1][0], w_ffn[1][1], 2)

    yp = xs[ctx].reshape(batch, seq, D_MODEL)
    ys = xs[lat].reshape(dec_batch, dec_seq, D_MODEL)
    new_state_gla = jnp.stack(gla_states, axis=1).astype(x_prompt.dtype)
    new_cache_mla_ckv = jnp.stack(mla_ckvs, axis=1)
    new_cache_mla_krope = jnp.stack(mla_krs, axis=1)
    return (yp, ys, new_state_gla, new_cache_mla_ckv, new_cache_mla_krope)
```

```python
import functools

import jax
import jax.numpy as jnp
import numpy as np
from jax import lax
from jax.experimental import pallas as pl
from jax.experimental.pallas import tpu as pltpu

F32 = jnp.float32
BF16 = jnp.bfloat16

D_MODEL = 1024
DEPTH = 2
GRID_W = 64
D_FF = 2816
MACARON_W = 0.5
N_MOD = 9
EPS = 1e-6

GLA_HEADS = 4
GLA_DK = 128
GLA_DV = 256
GLA_GATE_RANK = 16
GLA_TAU = 16.0
GLA_QK = GLA_HEADS * GLA_DK
GLA_VV = GLA_HEADS * GLA_DV

MLA_HEADS = 16
MLA_NOPE = 128
MLA_ROPE = 64
MLA_V = 128
MLA_Q_RANK = 512
MLA_KV_RANK = 256
MLA_SCALE = (MLA_NOPE + MLA_ROPE) ** -0.5
ROPE_BASE = 10000.0

VMEM_LIMIT_BYTES = 56 * 1024 * 1024

TOKEN_TILE = 512
FFN_CHUNK = 1408
GLA_CHUNK = 64
GLA_BLOCK = 512
MLA_Q_SUB = 256
MLA_KEY_TILE = 512
MLA_UNROLL = 4


def _params(*sem):
    return pltpu.CompilerParams(dimension_semantics=sem, vmem_limit_bytes=VMEM_LIMIT_BYTES)


def _resident(shape):
    nd = len(shape)
    return pl.BlockSpec(shape, lambda *_: (0,) * nd, pipeline_mode=pl.Buffered(1))


def _silu(x):
    return x * (1.0 / (1.0 + jnp.exp(-x)))


def _rms(x, g):
    return x * lax.rsqrt(jnp.mean(x * x, axis=-1, keepdims=True) + EPS) * g


def _pre(x, mod_ref, gn_ref, s):
    return _rms(x, gn_ref[0:1, :]) * (1.0 + mod_ref[3 * s + 1]) + mod_ref[3 * s]


def _residual(x, y, mod_ref, gn_ref, s, weight):
    return x + (weight * mod_ref[3 * s + 2]) * _rms(y, gn_ref[1:2, :])


def _dot(a, b):
    return jnp.dot(a, b, preferred_element_type=F32)


def _dot_nt(a, b):
    return lax.dot_general(a, b, (((1,), (1,)), ((), ())), preferred_element_type=F32)


def _dot_tn(a, b):
    return lax.dot_general(a, b, (((0,), (0,)), ((), ())), preferred_element_type=F32)


def _mod_kernel(cond_ref, w_ref, b_ref, o_ref):
    a = _silu(cond_ref[...]).astype(BF16)
    o_ref[...] = _dot(a, w_ref[...].astype(BF16)) + b_ref[...]


def _modulation(cond8, w_mod, b_mod):
    out = pl.pallas_call(
        _mod_kernel,
        grid=(DEPTH, N_MOD),
        in_specs=[
            pl.BlockSpec((8, D_MODEL), lambda l, j: (0, 0)),
            pl.BlockSpec((None, D_MODEL, D_MODEL), lambda l, j: (l, 0, j)),
            pl.BlockSpec((None, None, 1, D_MODEL), lambda l, j: (l, j, 0, 0)),
        ],
        out_specs=pl.BlockSpec((None, None, 8, D_MODEL), lambda l, j: (l, j, 0, 0)),
        out_shape=jax.ShapeDtypeStruct((DEPTH, N_MOD, 8, D_MODEL), F32),
        compiler_params=_params("parallel", "parallel"),
        name="modulation",
    )(cond8, w_mod, b_mod.reshape(DEPTH, N_MOD, 1, D_MODEL))
    return out.reshape(DEPTH, N_MOD, 8, 1, D_MODEL)


class _Stream:
    def __init__(self, batch, seq, mod_row0, per_batch_mod):
        self.batch, self.seq = batch, seq
        self.n = batch * seq
        self.tiles = self.n // TOKEN_TILE
        tiles_per_batch = seq // TOKEN_TILE
        if per_batch_mod:
            self.mod_row = lambda i: mod_row0 + i // tiles_per_batch
        else:
            self.mod_row = lambda i: mod_row0

    def tok(self, width):
        return pl.BlockSpec((TOKEN_TILE, width), lambda i: (i, 0))

    def mod(self):
        return pl.BlockSpec((N_MOD, None, 1, D_MODEL), lambda i: (0, self.mod_row(i), 0, 0))


def _ffn_kernel(x_ref, mod_ref, gn_ref, win_ref, wout_ref, o_ref, *, s):
    x = x_ref[...]
    h = _pre(x, mod_ref, gn_ref, s).astype(BF16)
    y = jnp.zeros((TOKEN_TILE, D_MODEL), F32)
    for lo in range(0, D_FF, FFN_CHUNK):
        g = _dot(h, win_ref[:, lo:lo + FFN_CHUNK])
        u = _dot(h, win_ref[:, D_FF + lo:D_FF + lo + FFN_CHUNK])
        a = (_silu(g) * u).astype(BF16)
        y = y + _dot(a, wout_ref[lo:lo + FFN_CHUNK, :])
    o_ref[...] = _residual(x, y, mod_ref, gn_ref, s, MACARON_W)


def _ffn(st, x, mods, gn, w_in, w_out, layer, which, s):
    once = pl.Buffered(1)
    return pl.pallas_call(
        functools.partial(_ffn_kernel, s=s),
        grid=(st.tiles,),
        in_specs=[st.tok(D_MODEL), st.mod(), _resident((2, D_MODEL)),
                  pl.BlockSpec((None, None, D_MODEL, 2 * D_FF), lambda i: (layer, which, 0, 0), pipeline_mode=once),
                  pl.BlockSpec((None, None, D_FF, D_MODEL), lambda i: (layer, which, 0, 0), pipeline_mode=once)],
        out_specs=st.tok(D_MODEL),
        out_shape=jax.ShapeDtypeStruct((st.n, D_MODEL), F32),
        compiler_params=_params("parallel"),
        name="ffn",
    )(x, mods, gn, w_in, w_out)


def _gla_proj_kernel(x_ref, mod_ref, gn_ref, w_ref, wz_ref, wg_ref, bg_ref,
                     q_ref, k_ref, v_ref, r_ref, la_ref):
    h = _pre(x_ref[...], mod_ref, gn_ref, 1).astype(BF16)
    q_ref[...] = _dot(h, w_ref[:, 0:GLA_QK]) * (GLA_DK ** -0.5)
    k_ref[...] = _dot(h, w_ref[:, GLA_QK:2 * GLA_QK])
    v_ref[...] = _dot(h, w_ref[:, 2 * GLA_QK:2 * GLA_QK + GLA_VV]).astype(BF16)
    r_ref[...] = _dot(h, w_ref[:, 2 * GLA_QK + GLA_VV:2 * GLA_QK + 2 * GLA_VV]).astype(BF16)
    z = _dot(h, wz_ref[...]).astype(BF16)
    logit = _dot(z, wg_ref[...]) + bg_ref[...]
    la_ref[...] = (jnp.minimum(logit, 0.0) - jnp.log(1.0 + jnp.exp(-jnp.abs(logit)))) * (1.0 / GLA_TAU)


def _gla_proj(st, x, mods, gn, w_qkvr, w_z, w_gate_bd, b_gate):
    n = st.n
    return pl.pallas_call(
        _gla_proj_kernel,
        grid=(st.tiles,),
        in_specs=[st.tok(D_MODEL), st.mod(), _resident((2, D_MODEL)),
                  _resident(w_qkvr.shape), _resident(w_z.shape),
                  _resident(w_gate_bd.shape), _resident(b_gate.shape)],
        out_specs=[st.tok(GLA_QK), st.tok(GLA_QK), st.tok(GLA_VV), st.tok(GLA_VV), st.tok(2 * GLA_QK)],
        out_shape=[jax.ShapeDtypeStruct((n, GLA_QK), F32), jax.ShapeDtypeStruct((n, GLA_QK), F32),
                   jax.ShapeDtypeStruct((n, GLA_VV), BF16), jax.ShapeDtypeStruct((n, GLA_VV), BF16),
                   jax.ShapeDtypeStruct((n, 2 * GLA_QK), F32)],
        compiler_params=_params("parallel"),
        name="gla_proj",
    )(x, mods, gn, w_qkvr, w_z, w_gate_bd, b_gate)


def _gla_scan_kernel(*refs, has_state, want_state):
    it = iter(refs)
    q_ref, k_ref, v_ref, la_ref = next(it), next(it), next(it), next(it)
    s0_ref = next(it) if has_state else None
    o_ref = next(it)
    sout_ref = next(it) if want_state else None
    st_ref = next(it)

    d = pl.program_id(1)
    blk = pl.program_id(2)
    n_chunks = q_ref.shape[0] // GLA_CHUNK
    head_k = [slice(h * GLA_DK, (h + 1) * GLA_DK) for h in range(GLA_HEADS)]
    head_v = [slice(h * GLA_DV, (h + 1) * GLA_DV) for h in range(GLA_HEADS)]

    @pl.when(blk == 0)
    def _():
        if has_state:
            st_ref[...] = s0_ref[...]
        else:
            st_ref[...] = jnp.zeros(st_ref.shape, F32)

    ii = lax.broadcasted_iota(jnp.int32, (GLA_CHUNK, GLA_CHUNK), 0)
    jj = lax.broadcasted_iota(jnp.int32, (GLA_CHUNK, GLA_CHUNK), 1)

    row = lax.broadcasted_iota(jnp.int32, (GLA_CHUNK, GLA_QK), 0)

    def cumsum(x, backward):
        step = 1
        while step < GLA_CHUNK:
            if backward:
                x = x + jnp.where(row < GLA_CHUNK - step, pltpu.roll(x, GLA_CHUNK - step, axis=0), 0.0)
            else:
                x = x + jnp.where(row >= step, pltpu.roll(x, step, axis=0), 0.0)
            step *= 2
        return x

    def run(backward):
        mask = (jj >= ii) if backward else (jj <= ii)
        order = list(range(n_chunks - 1, -1, -1) if backward else range(n_chunks))

        def prep(c):
            rows = slice(c * GLA_CHUNK, (c + 1) * GLA_CHUNK)
            la = la_ref[rows, :]
            b = cumsum(la, backward)
            btot = jnp.sum(la, axis=0, keepdims=True)
            bm = b[GLA_CHUNK // 2:GLA_CHUNK // 2 + 1, :]
            q = q_ref[rows, :]
            k = k_ref[rows, :]
            qm = (q * jnp.exp(b - bm)).astype(BF16)
            km = (k * jnp.exp(bm - b)).astype(BF16)
            att = [jnp.where(mask, _dot_nt(qm[:, sk], km[:, sk]), 0.0).astype(BF16) for sk in head_k]
            return dict(rows=rows, att=att, v=v_ref[rows, :], qb=(q * jnp.exp(b)).astype(BF16),
                        kt=(k * jnp.exp(btot - b)).astype(BF16), dec=jnp.exp(btot))

        def update(p):
            for h, (sk, sv) in enumerate(zip(head_k, head_v)):
                s = st_ref[h]
                v = p["v"][:, sv]
                o_ref[p["rows"], sv] = (_dot(p["att"][h], v) + _dot(p["qb"][:, sk], s.astype(BF16))).astype(BF16)
                dec_t = jnp.broadcast_to(p["dec"][:, sk], (GLA_DK, GLA_DK)).T
                st_ref[h] = s * jnp.concatenate([dec_t] * (GLA_DV // GLA_DK), axis=1) + _dot_tn(p["kt"][:, sk], v)

        nxt = prep(order[0])
        for n in range(n_chunks):
            cur, nxt = nxt, (prep(order[n + 1]) if n + 1 < n_chunks else None)
            update(cur)

    @pl.when(d == 0)
    def _():
        run(False)

    @pl.when(d == 1)
    def _():
        run(True)

    if want_state:
        @pl.when(blk == pl.num_programs(2) - 1)
        def _():
            sout_ref[...] = st_ref[...]


def _gla_scan(st, q, k, v, la, s0, want_state):
    block = min(GLA_BLOCK, st.seq)
    nb = st.seq // block
    has_state = s0 is not None

    def row(b, d, i):
        return b * nb + i + d * (nb - 1 - 2 * i)

    state_spec = pl.BlockSpec((None, None, GLA_HEADS, GLA_DK, GLA_DV), lambda b, d, i: (b, d, 0, 0, 0))
    in_specs = [pl.BlockSpec((block, GLA_QK), lambda b, d, i: (row(b, d, i), 0)),
                pl.BlockSpec((block, GLA_QK), lambda b, d, i: (row(b, d, i), 0)),
                pl.BlockSpec((block, GLA_VV), lambda b, d, i: (row(b, d, i), 0)),
                pl.BlockSpec((block, GLA_QK), lambda b, d, i: (row(b, d, i), d))]
    args = [q, k, v, la]
    if has_state:
        in_specs.append(state_spec)
        args.append(s0)
    out_specs = [pl.BlockSpec((None, block, GLA_VV), lambda b, d, i: (d, row(b, d, i), 0))]
    out_shape = [jax.ShapeDtypeStruct((2, st.n, GLA_VV), BF16)]
    if want_state:
        out_specs.append(state_spec)
        out_shape.append(jax.ShapeDtypeStruct((st.batch, 2, GLA_HEADS, GLA_DK, GLA_DV), F32))
    return pl.pallas_call(
        functools.partial(_gla_scan_kernel, has_state=has_state, want_state=want_state),
        grid=(st.batch, 2, nb),
        in_specs=in_specs,
        out_specs=out_specs,
        out_shape=out_shape,
        scratch_shapes=[pltpu.VMEM((GLA_HEADS, GLA_DK, GLA_DV), F32)],
        compiler_params=_params("parallel", "parallel", "arbitrary"),
        name="gla_scan",
    )(*args)


def _gla_post_kernel(o_ref, r_ref, x_ref, mod_ref, gn_ref, go_ref, w_ref, out_ref):
    o = o_ref[0].astype(F32) + o_ref[1].astype(F32)
    heads = []
    for h in range(GLA_HEADS):
        sv = slice(h * GLA_DV, (h + 1) * GLA_DV)
        heads.append(_rms(o[:, sv], go_ref[:, sv]))
    g = (jnp.concatenate(heads, axis=1) * _silu(r_ref[...].astype(F32))).astype(BF16)
    out_ref[...] = _residual(x_ref[...], _dot(g, w_ref[...]), mod_ref, gn_ref, 1, 1.0)


def _gla_post(st, o, r, x, mods, gn, g_out, w_out):
    return pl.pallas_call(
        _gla_post_kernel,
        grid=(st.tiles,),
        in_specs=[pl.BlockSpec((2, TOKEN_TILE, GLA_VV), lambda i: (0, i, 0)),
                  st.tok(GLA_VV), st.tok(D_MODEL), st.mod(), _resident((2, D_MODEL)),
                  _resident((1, GLA_VV)), _resident((GLA_VV, D_MODEL))],
        out_specs=st.tok(D_MODEL),
        out_shape=jax.ShapeDtypeStruct((st.n, D_MODEL), F32),
        compiler_params=_params("parallel"),
        name="gla_post",
    )(o, r, x, mods, gn, g_out, w_out)


MLA_Q_SCALE = MLA_SCALE * 1.4426950408889634


def _mla_proj_kernel(x_ref, mod_ref, gn_ref, w_ref, gq_ref, gkv_ref, wuqt_ref, tabqt_ref,
                     qt_ref, ckv_ref, kr_ref):
    h = _pre(x_ref[...], mod_ref, gn_ref, 1).astype(BF16)
    p = _dot(h, w_ref[...])
    ckv_ref[...] = _rms(p[:, MLA_Q_RANK:MLA_Q_RANK + MLA_KV_RANK], gkv_ref[...])
    kr_ref[...] = p[:, MLA_Q_RANK + MLA_KV_RANK:]
    cqt = _rms(p[:, 0:MLA_Q_RANK], gq_ref[...]).T.astype(BF16)
    tab = tabqt_ref[...] * MLA_Q_SCALE
    for hd in range(MLA_HEADS):
        qt = _dot(wuqt_ref[hd], cqt)
        for j in range(TOKEN_TILE // MLA_Q_SUB):
            cols = slice(j * MLA_Q_SUB, (j + 1) * MLA_Q_SUB)
            qt_ref[j, hd] = jnp.concatenate([qt[0:MLA_NOPE, cols] * MLA_Q_SCALE, qt[MLA_NOPE:, cols] * tab[:, cols]],
                                            axis=0).astype(BF16)


def _mla_proj(st, x, mods, gn, w_in_ext, g_q, g_kv, w_uqt, tabqt):
    n = st.n
    n_sub = TOKEN_TILE // MLA_Q_SUB
    tab_tiles = tabqt.shape[1] // TOKEN_TILE
    return pl.pallas_call(
        _mla_proj_kernel,
        grid=(st.tiles,),
        in_specs=[st.tok(D_MODEL), st.mod(), _resident((2, D_MODEL)), _resident(w_in_ext.shape),
                  _resident((1, MLA_Q_RANK)), _resident((1, MLA_KV_RANK)), _resident(w_uqt.shape),
                  pl.BlockSpec((2 * MLA_ROPE, TOKEN_TILE), lambda i: (0, i % tab_tiles))],
        out_specs=[pl.BlockSpec((n_sub, MLA_HEADS, 256, MLA_Q_SUB), lambda i: (i, 0, 0, 0)),
                   st.tok(MLA_KV_RANK), st.tok(2 * MLA_ROPE)],
        out_shape=[jax.ShapeDtypeStruct((n // MLA_Q_SUB, MLA_HEADS, 256, MLA_Q_SUB), BF16),
                   jax.ShapeDtypeStruct((n, MLA_KV_RANK), F32),
                   jax.ShapeDtypeStruct((n, 2 * MLA_ROPE), F32)],
        compiler_params=_params("parallel"),
        name="mla_proj",
    )(x, mods, gn, w_in_ext, g_q, g_kv, w_uqt, tabqt)


def _mla_build_kv(ckv, kr_dup, wukv, k_dst, vt_dst):
    k_dst[:, MLA_NOPE:] = kr_dup
    for lo, hi in _mla_key_parts(ckv.shape[0]):
        kv = _dot(ckv[lo:hi, :], wukv)
        k_dst[lo:hi, 0:MLA_NOPE] = kv[:, 0:MLA_NOPE].astype(BF16)
        vt_dst[:, lo:hi] = kv[:, MLA_NOPE:].T.astype(BF16)


def _mla_key_parts(tk):
    half = (tk // 2 + 255) // 256 * 256
    return [(0, tk)] if tk <= 256 else [(0, half), (half, tk)]


def _mla_colmax(sts):
    return functools.reduce(jnp.maximum, [jnp.max(s, axis=0, keepdims=True) for s in sts])


def _mla_softmax_pv(sts, m, vt_parts):
    pts = [jnp.exp2(s - m) for s in sts]
    denom = functools.reduce(jnp.add, [jnp.sum(p, axis=0, keepdims=True) for p in pts])
    ot = functools.reduce(jnp.add, [_dot(vt, p.astype(BF16)) for vt, p in zip(vt_parts, pts)])
    return (ot / denom).T.astype(BF16)


def _mla_attn_kernel(qt_ref, ckv_ref, kr_ref, wukv_ref, o_ref, k_scr, vt_scr):
    kr = kr_ref[...]
    kr_dup = (kr + pltpu.roll(kr, MLA_ROPE, axis=1)).astype(BF16)
    ckv = ckv_ref[...]

    def scores(h):
        _mla_build_kv(ckv, kr_dup, wukv_ref[h], k_scr.at[h], vt_scr.at[h])
        return _dot(k_scr[h], qt_ref[h])

    nxt = scores(0)
    for h in range(MLA_HEADS):
        cur, nxt = nxt, (scores(h + 1) if h + 1 < MLA_HEADS else None)
        o_ref[:, h * MLA_V:(h + 1) * MLA_V] = _mla_softmax_pv([cur], _mla_colmax([cur]), [vt_scr[h]])


def _mla_attn(batch, seq, qt, ckv3, kr_masked, w_ukv):
    return pl.pallas_call(
        _mla_attn_kernel,
        grid=(batch,),
        in_specs=[pl.BlockSpec((None, MLA_HEADS, 256, seq), lambda b: (b, 0, 0, 0)),
                  pl.BlockSpec((None, seq, MLA_KV_RANK), lambda b: (b, 0, 0)),
                  pl.BlockSpec((None, seq, 2 * MLA_ROPE), lambda b: (b, 0, 0)),
                  _resident(w_ukv.shape)],
        out_specs=pl.BlockSpec((seq, MLA_HEADS * MLA_V), lambda b: (b, 0)),
        out_shape=jax.ShapeDtypeStruct((batch * seq, MLA_HEADS * MLA_V), BF16),
        scratch_shapes=[pltpu.VMEM((MLA_HEADS, seq, 256), BF16), pltpu.VMEM((MLA_HEADS, MLA_V, seq), BF16)],
        compiler_params=_params("parallel"),
        name="mla_attn",
    )(qt, ckv3, kr_masked, w_ukv)


def _mla_attn_long_kernel(qt_ref, ckv_ref, kr_ref, tabk_ref, wukv_ref, o_ref,
                          k_scr, vt_scr, st_scr, m_scr, acc_scr, den_scr):
    n_sub = qt_ref.shape[0]
    tk = k_scr.shape[0]
    tiles = [(lo, min(lo + MLA_KEY_TILE, tk)) for lo in range(0, tk, MLA_KEY_TILE)]
    halves = [[(lo, (lo + hi) // 2), ((lo + hi) // 2, hi)] for lo, hi in tiles]

    y = kr_ref[...] * tabk_ref[...]
    _mla_build_kv(ckv_ref[...], (y + pltpu.roll(y, MLA_ROPE, axis=1)).astype(BF16), wukv_ref[...], k_scr, vt_scr)

    def fold8(x, op):
        return op(x.reshape(x.shape[0] // 8, 8, x.shape[1]), axis=0)

    def finalize(j, slot):
        rows = pl.ds(pl.multiple_of(j * MLA_Q_SUB, MLA_Q_SUB), MLA_Q_SUB)
        o_ref[rows, :] = (acc_scr[slot] / jnp.sum(den_scr[slot], axis=0, keepdims=True)).T.astype(BF16)

    def stage(j_next, slot_next, j_cur, slot_cur, j_done):
        qt = None if j_next is None else qt_ref[j_next]
        if j_cur is not None:
            m = jnp.max(m_scr[slot_cur], axis=0, keepdims=True)
            acc = jnp.zeros((MLA_V, MLA_Q_SUB), F32)
            den = jnp.zeros((8, MLA_Q_SUB), F32)
        run_max = None
        for n, pieces in enumerate(halves):
            for lo, hi in pieces:
                if qt is not None:
                    s = _dot(k_scr[lo:hi, :], qt)
                    st_scr[slot_next, lo:hi, :] = s
                    s8 = fold8(s, jnp.max)
                    run_max = s8 if run_max is None else jnp.maximum(run_max, s8)
            for lo, hi in pieces:
                if j_cur is not None:
                    p = jnp.exp2(st_scr[slot_cur, lo:hi, :] - m)
                    den = den + fold8(p, jnp.sum)
                    acc = acc + _dot(vt_scr[:, lo:hi], p.astype(BF16))
            if n == 0 and j_done is not None:
                finalize(j_done, 1 - slot_cur)
        if qt is not None:
            m_scr[slot_next] = run_max
        if j_cur is not None:
            acc_scr[slot_cur] = acc
            den_scr[slot_cur] = den

    acc_scr[1] = jnp.zeros(acc_scr.shape[1:], F32)
    den_scr[1] = jnp.ones(den_scr.shape[1:], F32)
    stage(0, 0, None, None, None)

    def quad(t, carry):
        j = MLA_UNROLL * t
        for i in range(MLA_UNROLL):
            stage(jnp.minimum(j + i + 1, n_sub - 1), (i + 1) % 2, j + i, i % 2, jnp.maximum(j + i - 1, 0))
        return carry

    lax.fori_loop(0, n_sub // MLA_UNROLL, quad, 0)
    finalize(n_sub - 1, (n_sub - 1) % 2)


def _mla_attn_long(batch, tq_total, qt, ckv_all, kr_all, tabk, w_ukv):
    tk = ckv_all.shape[1]
    n_sub = tq_total // MLA_Q_SUB
    once = pl.Buffered(1)
    return pl.pallas_call(
        _mla_attn_long_kernel,
        grid=(batch, MLA_HEADS),
        in_specs=[pl.BlockSpec((n_sub, None, 256, MLA_Q_SUB), lambda b, h: (b, h, 0, 0)),
                  pl.BlockSpec((None, tk, MLA_KV_RANK), lambda b, h: (b, 0, 0), pipeline_mode=once),
                  pl.BlockSpec((None, tk, 2 * MLA_ROPE), lambda b, h: (b, 0, 0), pipeline_mode=once),
                  pl.BlockSpec((tk, 2 * MLA_ROPE), lambda b, h: (0, 0), pipeline_mode=once),
                  pl.BlockSpec((None, MLA_KV_RANK, 256), lambda b, h: (h, 0, 0))],
        out_specs=pl.BlockSpec((tq_total, MLA_V), lambda b, h: (b, h)),
        out_shape=jax.ShapeDtypeStruct((batch * tq_total, MLA_HEADS * MLA_V), BF16),
        scratch_shapes=[pltpu.VMEM((tk, 256), BF16), pltpu.VMEM((MLA_V, tk), BF16),
                        pltpu.VMEM((2, tk, MLA_Q_SUB), F32), pltpu.VMEM((2, 8, MLA_Q_SUB), F32),
                        pltpu.VMEM((2, MLA_V, MLA_Q_SUB), F32), pltpu.VMEM((2, 8, MLA_Q_SUB), F32)],
        compiler_params=_params("parallel", "arbitrary"),
        name="mla_attn_long",
    )(qt, ckv_all, kr_all, tabk, w_ukv)


def _mla_post_kernel(a_ref, x_ref, mod_ref, gn_ref, w_ref, out_ref):
    out_ref[...] = _residual(x_ref[...], _dot(a_ref[...], w_ref[...]), mod_ref, gn_ref, 1, 1.0)


def _mla_post(st, a, x, mods, gn, w_out):
    return pl.pallas_call(
        _mla_post_kernel,
        grid=(st.tiles,),
        in_specs=[st.tok(MLA_HEADS * MLA_V), st.tok(D_MODEL), st.mod(), _resident((2, D_MODEL)),
                  _resident((MLA_HEADS * MLA_V, D_MODEL))],
        out_specs=st.tok(D_MODEL),
        out_shape=jax.ShapeDtypeStruct((st.n, D_MODEL), F32),
        compiler_params=_params("parallel"),
        name="mla_post",
    )(a, x, mods, gn, w_out)


_ROPE_SWAP = tuple(list(range(16, 32)) + list(range(0, 16)) + list(range(48, 64)) + list(range(32, 48)))


def _rope_table(n_tokens):
    rows = n_tokens // GRID_W
    row = np.repeat(np.arange(rows), GRID_W).astype(np.float64)
    col = np.tile(np.arange(GRID_W), rows).astype(np.float64)
    n_pairs = MLA_ROPE // 4
    inv = ROPE_BASE ** (-np.arange(n_pairs, dtype=np.float64) / n_pairs)
    ar, ac = row[:, None] * inv, col[:, None] * inv
    cos = np.concatenate([np.cos(ar), np.cos(ar), np.cos(ac), np.cos(ac)], axis=1)
    sin = np.concatenate([-np.sin(ar), np.sin(ar), -np.sin(ac), np.sin(ac)], axis=1)
    return np.concatenate([cos, sin], axis=1).astype(np.float32)


def _identity_table(n_tokens):
    return np.concatenate([np.ones((n_tokens, MLA_ROPE), np.float32), np.zeros((n_tokens, MLA_ROPE), np.float32)], axis=1)


def kernel(x_prompt, x_sample, state_gla, cache_mla_ckv, cache_mla_krope, c, c_ctx, w_mod, b_mod, g_norm, w_ffn_in, w_ffn_out, gla_w_in, gla_w_gate, gla_b_gate, gla_g_out, gla_w_out, mla_w_in, mla_g_q, mla_g_kv, mla_w_uq, mla_w_ukv, mla_w_out):
    batch, seq = x_prompt.shape[0], x_prompt.shape[1]
    dec_batch, dec_seq = x_sample.shape[0], x_sample.shape[1]
    past = cache_mla_ckv.shape[2]
    ctx = _Stream(batch, seq, dec_batch, per_batch_mod=False)
    lat = _Stream(dec_batch, dec_seq, 0, per_batch_mod=True)

    cond8 = jnp.concatenate([c, c_ctx[None, :], jnp.zeros((8 - dec_batch - 1, D_MODEL), F32)], axis=0)
    mods = _modulation(cond8, w_mod, b_mod)

    xs = {ctx: x_prompt.reshape(ctx.n, D_MODEL), lat: x_sample.reshape(lat.n, D_MODEL)}
    w_ffn_in_b = w_ffn_in.astype(BF16)
    w_ffn_out_b = w_ffn_out.astype(BF16)
    gla_states, mla_ckvs, mla_krs = [], [], []

    for l in range(DEPTH):
        j = l // 2
        for st in (ctx, lat):
            xs[st] = _ffn(st, xs[st], mods[l], g_norm[l, 0], w_ffn_in_b, w_ffn_out_b, l, 0, 0)

        if l % 2 == 0:
            w_in = gla_w_in[j].astype(BF16)
            w_qkvr, w_z = w_in[:, :2 * GLA_QK + 2 * GLA_VV], w_in[:, 2 * GLA_QK + 2 * GLA_VV:]
            zeros = jnp.zeros((GLA_GATE_RANK, GLA_QK), F32)
            w_gate_bd = jnp.concatenate([jnp.concatenate([gla_w_gate[j, 0], zeros], axis=1),
                                         jnp.concatenate([zeros, gla_w_gate[j, 1]], axis=1)], axis=0).astype(BF16)
            b_gate = gla_b_gate[j].reshape(1, 2 * GLA_QK)
            g_out = gla_g_out[j].reshape(1, GLA_VV)
            w_out = gla_w_out[j].astype(BF16)
            for st in (ctx, lat):
                q, k, v, r, la = _gla_proj(st, xs[st], mods[l], g_norm[l, 1], w_qkvr, w_z, w_gate_bd, b_gate)
                if st is ctx:
                    o, s_fin = _gla_scan(st, q, k, v, la, None, True)
                    gla_states.append(s_fin)
                else:
                    (o,) = _gla_scan(st, q, k, v, la, state_gla[:, j].astype(F32), False)
                xs[st] = _gla_post(st, o, r, xs[st], mods[l], g_norm[l, 1], g_out, w_out)
        else:
            swap = jnp.array(_ROPE_SWAP)
            w_in = mla_w_in[j]
            w_in_ext = jnp.concatenate([w_in, w_in[:, MLA_Q_RANK + MLA_KV_RANK:][:, swap]], axis=1).astype(BF16)
            w_uq = mla_w_uq[j].reshape(MLA_Q_RANK, MLA_HEADS, MLA_NOPE + MLA_ROPE)
            w_uq = jnp.concatenate([w_uq, w_uq[:, :, MLA_NOPE:][:, :, swap]], axis=2)
            w_uqt = jnp.transpose(w_uq, (1, 2, 0)).astype(BF16)
            w_ukv = jnp.transpose(mla_w_ukv[j].reshape(MLA_KV_RANK, MLA_HEADS, MLA_NOPE + MLA_V), (1, 0, 2)).astype(BF16)
            g_q = mla_g_q[j].reshape(1, MLA_Q_RANK)
            g_kv = mla_g_kv[j].reshape(1, MLA_KV_RANK)
            w_out = mla_w_out[j].astype(BF16)
            tabq = _rope_table(lat.seq)
            tabqt = {ctx: jnp.asarray(_identity_table(TOKEN_TILE).T), lat: jnp.asarray(tabq.T)}
            for st in (ctx, lat):
                qt, ckv, kr2 = _mla_proj(st, xs[st], mods[l], g_norm[l, 1], w_in_ext, g_q, g_kv, w_uqt, tabqt[st])
                ckv3 = ckv.reshape(st.batch, st.seq, MLA_KV_RANK)
                kr3 = kr2.reshape(st.batch, st.seq, 2 * MLA_ROPE)
                if st is ctx:
                    mla_ckvs.append(ckv3)
                    kr = kr3[:, :, :MLA_ROPE]
                    mla_krs.append(kr)
                    kr_masked = jnp.concatenate([kr, jnp.zeros_like(kr)], axis=2)
                    a = _mla_attn(st.batch, st.seq, qt, ckv3.astype(BF16), kr_masked, w_ukv)
                else:
                    ckv_all = jnp.concatenate([cache_mla_ckv[:, j].astype(F32), ckv3], axis=1).astype(BF16)
                    kr_ctx = cache_mla_krope[:, j].astype(F32)
                    kr_all = jnp.concatenate([jnp.concatenate([kr_ctx, jnp.zeros_like(kr_ctx)], axis=2), kr3], axis=1)
                    tabk = jnp.asarray(np.concatenate([_identity_table(past), tabq], axis=0))
                    a = _mla_attn_long(st.batch, st.seq, qt, ckv_all, kr_all, tabk, w_ukv)
                xs[st] = _mla_post(st, a, xs[st], mods[l], g_norm[l, 1], w_out)

        for st in (ctx, lat):
            xs[st] = _ffn(st, xs[st], mods[l], g_norm[l, 2], w_ffn_in_b, w_ffn_out_b, l, 1, 2)

    yp = xs[ctx].reshape(batch, seq, D_MODEL)
    ys = xs[lat].reshape(dec_batch, dec_seq, D_MODEL)
    new_state_gla = jnp.stack(gla_states, axis=1).astype(x_prompt.dtype)
    new_cache_mla_ckv = jnp.stack(mla_ckvs, axis=1)
    new_cache_mla_krope = jnp.stack(mla_krs, axis=1)
    return (yp, ys, new_state_gla, new_cache_mla_ckv, new_cache_mla_krope)
```

```python
import functools

import jax
import jax.numpy as jnp
import numpy as np
from jax import lax
from jax.experimental import pallas as pl
from jax.experimental.pallas import tpu as pltpu

F32 = jnp.float32
BF16 = jnp.bfloat16

D_MODEL = 1024
DEPTH = 2
GRID_W = 64
D_FF = 2816
MACARON_W = 0.5
N_MOD = 9
EPS = 1e-6
LOG2_E = 1.4426950408889634

GLA_HEADS = 4
GLA_DK = 128
GLA_DV = 256
GLA_GATE_RANK = 16
GLA_TAU = 16.0
GLA_QK = GLA_HEADS * GLA_DK
GLA_VV = GLA_HEADS * GLA_DV

MLA_HEADS = 16
MLA_NOPE = 128
MLA_ROPE = 64
MLA_V = 128
MLA_Q_RANK = 512
MLA_KV_RANK = 256
MLA_SCALE = (MLA_NOPE + MLA_ROPE) ** -0.5
ROPE_BASE = 10000.0

VMEM_LIMIT_BYTES = 56 * 1024 * 1024

TOKEN_TILE = 512
FFN_CHUNK = 1408
GLA_CHUNK = 64
GLA_BLOCK = 512
GLA_SAFE_LOG2 = 120.0
MLA_Q_SUB = 256
MLA_KEY_TILE = 512
MLA_UNROLL = 4


def _params(*sem):
    return pltpu.CompilerParams(dimension_semantics=sem, vmem_limit_bytes=VMEM_LIMIT_BYTES)


def _resident(shape):
    nd = len(shape)
    return pl.BlockSpec(shape, lambda *_: (0,) * nd, pipeline_mode=pl.Buffered(1))


def _silu(x):
    return x * (1.0 / (1.0 + jnp.exp(-x)))


def _rms(x, g):
    return x * lax.rsqrt(jnp.mean(x * x, axis=-1, keepdims=True) + EPS) * g


def _pre(x, mod_ref, gn_ref, s):
    return _rms(x, gn_ref[0:1, :]) * (1.0 + mod_ref[3 * s + 1]) + mod_ref[3 * s]


def _residual(x, y, mod_ref, gn_ref, s, weight):
    return x + (weight * mod_ref[3 * s + 2]) * _rms(y, gn_ref[1:2, :])


def _dot(a, b):
    return jnp.dot(a, b, preferred_element_type=F32)


def _dot_nt(a, b):
    return lax.dot_general(a, b, (((1,), (1,)), ((), ())), preferred_element_type=F32)


def _dot_tn(a, b):
    return lax.dot_general(a, b, (((0,), (0,)), ((), ())), preferred_element_type=F32)


def _mod_kernel(cond_ref, w_ref, b_ref, o_ref):
    a = _silu(cond_ref[...]).astype(BF16)
    o_ref[...] = _dot(a, w_ref[...].astype(BF16)) + b_ref[...]


def _modulation(cond8, w_mod, b_mod):
    out = pl.pallas_call(
        _mod_kernel,
        grid=(DEPTH, N_MOD),
        in_specs=[
            pl.BlockSpec((8, D_MODEL), lambda l, j: (0, 0)),
            pl.BlockSpec((None, D_MODEL, D_MODEL), lambda l, j: (l, 0, j)),
            pl.BlockSpec((None, None, 1, D_MODEL), lambda l, j: (l, j, 0, 0)),
        ],
        out_specs=pl.BlockSpec((None, None, 8, D_MODEL), lambda l, j: (l, j, 0, 0)),
        out_shape=jax.ShapeDtypeStruct((DEPTH, N_MOD, 8, D_MODEL), F32),
        compiler_params=_params("parallel", "parallel"),
        name="modulation",
    )(cond8, w_mod, b_mod.reshape(DEPTH, N_MOD, 1, D_MODEL))
    return out.reshape(DEPTH, N_MOD, 8, 1, D_MODEL)


class _Stream:
    def __init__(self, batch, seq, mod_row0, per_batch_mod):
        self.batch, self.seq = batch, seq
        self.n = batch * seq
        self.tiles = self.n // TOKEN_TILE
        tiles_per_batch = seq // TOKEN_TILE
        if per_batch_mod:
            self.mod_row = lambda i: mod_row0 + i // tiles_per_batch
        else:
            self.mod_row = lambda i: mod_row0

    def tok(self, width):
        return pl.BlockSpec((TOKEN_TILE, width), lambda i: (i, 0))

    def mod(self):
        return pl.BlockSpec((N_MOD, None, 1, D_MODEL), lambda i: (0, self.mod_row(i), 0, 0))


def _ffn_kernel(x_ref, mod_ref, gn_ref, win_ref, wout_ref, o_ref, *, s):
    x = x_ref[...]
    h = _pre(x, mod_ref, gn_ref, s).astype(BF16)
    y = jnp.zeros((TOKEN_TILE, D_MODEL), F32)
    for lo in range(0, D_FF, FFN_CHUNK):
        g = _dot(h, win_ref[:, lo:lo + FFN_CHUNK])
        u = _dot(h, win_ref[:, D_FF + lo:D_FF + lo + FFN_CHUNK])
        a = (_silu(g) * u).astype(BF16)
        y = y + _dot(a, wout_ref[lo:lo + FFN_CHUNK, :])
    o_ref[...] = _residual(x, y, mod_ref, gn_ref, s, MACARON_W)


def _ffn(st, x, mods, gn, w_in, w_out, layer, which, s):
    once = pl.Buffered(1)
    return pl.pallas_call(
        functools.partial(_ffn_kernel, s=s),
        grid=(st.tiles,),
        in_specs=[st.tok(D_MODEL), st.mod(), _resident((2, D_MODEL)),
                  pl.BlockSpec((None, None, D_MODEL, 2 * D_FF), lambda i: (layer, which, 0, 0), pipeline_mode=once),
                  pl.BlockSpec((None, None, D_FF, D_MODEL), lambda i: (layer, which, 0, 0), pipeline_mode=once)],
        out_specs=st.tok(D_MODEL),
        out_shape=jax.ShapeDtypeStruct((st.n, D_MODEL), F32),
        compiler_params=_params("parallel"),
        name="ffn",
    )(x, mods, gn, w_in, w_out)


def _gla_proj_kernel(x_ref, mod_ref, gn_ref, w_ref, wz_ref, wg_ref, bg_ref,
                     q_ref, k_ref, v_ref, r_ref, la_ref):
    h = _pre(x_ref[...], mod_ref, gn_ref, 1).astype(BF16)
    z = _dot(h, wz_ref[...]).astype(BF16)
    r_ref[...] = _silu(_dot(h, w_ref[:, 2 * GLA_QK + GLA_VV:2 * GLA_QK + 2 * GLA_VV])).astype(BF16)
    logit = _dot(z, wg_ref[...]) + bg_ref[...]
    la_ref[...] = (jnp.minimum(logit, 0.0) - jnp.log(1.0 + jnp.exp(-jnp.abs(logit)))) * (LOG2_E / GLA_TAU)
    q_ref[...] = _dot(h, w_ref[:, 0:GLA_QK]) * (GLA_DK ** -0.5)
    k_ref[...] = _dot(h, w_ref[:, GLA_QK:2 * GLA_QK])
    v_ref[...] = _dot(h, w_ref[:, 2 * GLA_QK:2 * GLA_QK + GLA_VV]).astype(BF16)


def _gla_proj(st, x, mods, gn, w_qkvr, w_z, w_gate_bd, b_gate):
    n = st.n
    return pl.pallas_call(
        _gla_proj_kernel,
        grid=(st.tiles,),
        in_specs=[st.tok(D_MODEL), st.mod(), _resident((2, D_MODEL)),
                  _resident(w_qkvr.shape), _resident(w_z.shape),
                  _resident(w_gate_bd.shape), _resident(b_gate.shape)],
        out_specs=[st.tok(GLA_QK), st.tok(GLA_QK), st.tok(GLA_VV), st.tok(GLA_VV), st.tok(2 * GLA_QK)],
        out_shape=[jax.ShapeDtypeStruct((n, GLA_QK), F32), jax.ShapeDtypeStruct((n, GLA_QK), F32),
                   jax.ShapeDtypeStruct((n, GLA_VV), BF16), jax.ShapeDtypeStruct((n, GLA_VV), BF16),
                   jax.ShapeDtypeStruct((n, 2 * GLA_QK), F32)],
        compiler_params=_params("parallel"),
        name="gla_proj",
    )(x, mods, gn, w_qkvr, w_z, w_gate_bd, b_gate)


def _gla_scan_kernel(*refs, has_state, want_state):
    it = iter(refs)
    q_ref, k_ref, v_ref, la_ref = next(it), next(it), next(it), next(it)
    s0_ref = next(it) if has_state else None
    o_ref = next(it)
    sout_ref = next(it) if want_state else None
    st_ref = next(it)

    d = pl.program_id(1)
    blk = pl.program_id(2)
    n_chunks = q_ref.shape[0] // GLA_CHUNK
    head_k = [slice(h * GLA_DK, (h + 1) * GLA_DK) for h in range(GLA_HEADS)]
    head_v = [slice(h * GLA_DV, (h + 1) * GLA_DV) for h in range(GLA_HEADS)]

    @pl.when(blk == 0)
    def _():
        if has_state:
            st_ref[...] = s0_ref[...]
        else:
            st_ref[...] = jnp.zeros(st_ref.shape, F32)

    ii = lax.broadcasted_iota(jnp.int32, (GLA_CHUNK, GLA_CHUNK), 0)
    jj = lax.broadcasted_iota(jnp.int32, (GLA_CHUNK, GLA_CHUNK), 1)

    n_groups = GLA_CHUNK // 8
    sub = lax.broadcasted_iota(jnp.int32, (n_groups, 8, GLA_QK), 1)

    def cumsum(x, backward):
        g = x.reshape(n_groups, 8, GLA_QK)
        for step in (1, 2, 4):
            if backward:
                g = g + jnp.where(sub < 8 - step, pltpu.roll(g, 8 - step, axis=1), 0.0)
            else:
                g = g + jnp.where(sub >= step, pltpu.roll(g, step, axis=1), 0.0)
        edge = 0 if backward else 7
        order = range(n_groups - 1, -1, -1) if backward else range(n_groups)
        out, carry = [None] * n_groups, None
        for i in order:
            out[i] = g[i] if carry is None else g[i] + carry
            carry = jnp.broadcast_to(out[i][edge:edge + 1, :], (8, GLA_QK))
        return jnp.concatenate(out, axis=0)

    def pairwise_scores(q, k, b, mask):
        lane = lax.broadcasted_iota(jnp.int32, (GLA_CHUNK, GLA_CHUNK), 1)
        rowi = lax.broadcasted_iota(jnp.int32, (GLA_CHUNK, GLA_DK), 0)
        out = []
        for sk in head_k:
            def key_row(j, att, sk=sk):
                pick = rowi == j
                bj = jnp.sum(jnp.where(pick, b[:, sk], 0.0), axis=0, keepdims=True)
                kj = jnp.sum(jnp.where(pick, k[:, sk], 0.0), axis=0, keepdims=True)
                col = jnp.sum(q[:, sk] * jnp.exp2(jnp.minimum(b[:, sk] - bj, 0.0)) * kj, axis=1, keepdims=True)
                return att + jnp.where(lane == j, col, 0.0)
            att = lax.fori_loop(0, GLA_CHUNK, key_row, jnp.zeros((GLA_CHUNK, GLA_CHUNK), F32))
            out.append(jnp.where(mask, att, 0.0).astype(BF16))
        return out

    def run(backward, factored):
        mask = (jj >= ii) if backward else (jj <= ii)
        order = list(range(n_chunks - 1, -1, -1) if backward else range(n_chunks))

        def prep(c):
            rows = slice(c * GLA_CHUNK, (c + 1) * GLA_CHUNK)
            la = la_ref[rows, :]
            b = cumsum(la, backward)
            btot = jnp.sum(la, axis=0, keepdims=True)
            q = q_ref[rows, :]
            k = k_ref[rows, :]
            if factored:
                bm = b[GLA_CHUNK // 2:GLA_CHUNK // 2 + 1, :]
                qm = (q * jnp.exp2(b - bm)).astype(BF16)
                km = (k * jnp.exp2(bm - b)).astype(BF16)
                att = [jnp.where(mask, _dot_nt(qm[:, sk], km[:, sk]), 0.0).astype(BF16) for sk in head_k]
            else:
                att = pairwise_scores(q, k, b, mask)
            return dict(rows=rows, att=att, v=v_ref[rows, :], qb=(q * jnp.exp2(b)).astype(BF16),
                        kt=(k * jnp.exp2(btot - b)).astype(BF16), dec=jnp.exp2(btot))

        def update(p):
            for h, (sk, sv) in enumerate(zip(head_k, head_v)):
                s = st_ref[h]
                v = p["v"][:, sv]
                o_ref[p["rows"], sv] = (_dot(p["att"][h], v) + _dot(p["qb"][:, sk], s.astype(BF16))).astype(BF16)
                dec_t = jnp.broadcast_to(p["dec"][:, sk], (GLA_DK, GLA_DK)).T
                st_ref[h] = s * jnp.concatenate([dec_t] * (GLA_DV // GLA_DK), axis=1) + _dot_tn(p["kt"][:, sk], v)

        nxt = prep(order[0])
        for n in range(n_chunks):
            cur, nxt = nxt, (prep(order[n + 1]) if n + 1 < n_chunks else None)
            update(cur)

    totals = jnp.sum(la_ref[...].reshape(n_chunks, GLA_CHUNK, GLA_QK), axis=1)
    safe = jnp.min(totals) > -GLA_SAFE_LOG2
    for backward in (False, True):
        for factored in (True, False):
            @pl.when(jnp.logical_and(d == int(backward), safe if factored else jnp.logical_not(safe)))
            def _(backward=backward, factored=factored):
                run(backward, factored)

    if want_state:
        @pl.when(blk == pl.num_programs(2) - 1)
        def _():
            sout_ref[...] = st_ref[...]


def _gla_scan(st, q, k, v, la, s0, want_state):
    block = min(GLA_BLOCK, st.seq)
    nb = st.seq // block
    has_state = s0 is not None

    def row(b, d, i):
        return b * nb + i + d * (nb - 1 - 2 * i)

    state_spec = pl.BlockSpec((None, None, GLA_HEADS, GLA_DK, GLA_DV), lambda b, d, i: (b, d, 0, 0, 0))
    in_specs = [pl.BlockSpec((block, GLA_QK), lambda b, d, i: (row(b, d, i), 0)),
                pl.BlockSpec((block, GLA_QK), lambda b, d, i: (row(b, d, i), 0)),
                pl.BlockSpec((block, GLA_VV), lambda b, d, i: (row(b, d, i), 0)),
                pl.BlockSpec((block, GLA_QK), lambda b, d, i: (row(b, d, i), d))]
    args = [q, k, v, la]
    if has_state:
        in_specs.append(state_spec)
        args.append(s0)
    out_specs = [pl.BlockSpec((None, block, GLA_VV), lambda b, d, i: (d, row(b, d, i), 0))]
    out_shape = [jax.ShapeDtypeStruct((2, st.n, GLA_VV), BF16)]
    if want_state:
        out_specs.append(state_spec)
        out_shape.append(jax.ShapeDtypeStruct((st.batch, 2, GLA_HEADS, GLA_DK, GLA_DV), F32))
    return pl.pallas_call(
        functools.partial(_gla_scan_kernel, has_state=has_state, want_state=want_state),
        grid=(st.batch, 2, nb),
        in_specs=in_specs,
        out_specs=out_specs,
        out_shape=out_shape,
        scratch_shapes=[pltpu.VMEM((GLA_HEADS, GLA_DK, GLA_DV), F32)],
        compiler_params=_params("parallel", "parallel", "arbitrary"),
        name="gla_scan",
    )(*args)


def _gla_post_kernel(o_ref, r_ref, x_ref, mod_ref, gn_ref, go_ref, w_ref, out_ref):
    o = o_ref[0].astype(F32) + o_ref[1].astype(F32)
    heads = []
    for h in range(GLA_HEADS):
        sv = slice(h * GLA_DV, (h + 1) * GLA_DV)
        heads.append(_rms(o[:, sv], go_ref[:, sv]))
    g = (jnp.concatenate(heads, axis=1) * r_ref[...].astype(F32)).astype(BF16)
    out_ref[...] = _residual(x_ref[...], _dot(g, w_ref[...]), mod_ref, gn_ref, 1, 1.0)


def _gla_post(st, o, r, x, mods, gn, g_out, w_out):
    return pl.pallas_call(
        _gla_post_kernel,
        grid=(st.tiles,),
        in_specs=[pl.BlockSpec((2, TOKEN_TILE, GLA_VV), lambda i: (0, i, 0)),
                  st.tok(GLA_VV), st.tok(D_MODEL), st.mod(), _resident((2, D_MODEL)),
                  _resident((1, GLA_VV)), _resident((GLA_VV, D_MODEL))],
        out_specs=st.tok(D_MODEL),
        out_shape=jax.ShapeDtypeStruct((st.n, D_MODEL), F32),
        compiler_params=_params("parallel"),
        name="gla_post",
    )(o, r, x, mods, gn, g_out, w_out)


MLA_Q_SCALE = MLA_SCALE * LOG2_E


def _mla_proj_kernel(x_ref, mod_ref, gn_ref, w_ref, gq_ref, gkv_ref, wuqt_ref, tabqt_ref,
                     qt_ref, ckv_ref, kr_ref):
    h = _pre(x_ref[...], mod_ref, gn_ref, 1).astype(BF16)
    p = _dot(h, w_ref[...])
    ckv_ref[...] = _rms(p[:, MLA_Q_RANK:MLA_Q_RANK + MLA_KV_RANK], gkv_ref[...])
    kr_ref[...] = p[:, MLA_Q_RANK + MLA_KV_RANK:]
    cqt = _rms(p[:, 0:MLA_Q_RANK], gq_ref[...]).T.astype(BF16)
    tab = tabqt_ref[...] * MLA_Q_SCALE
    for hd in range(MLA_HEADS):
        qt = _dot(wuqt_ref[hd], cqt)
        for j in range(TOKEN_TILE // MLA_Q_SUB):
            cols = slice(j * MLA_Q_SUB, (j + 1) * MLA_Q_SUB)
            qt_ref[j, hd] = jnp.concatenate([qt[0:MLA_NOPE, cols] * MLA_Q_SCALE, qt[MLA_NOPE:, cols] * tab[:, cols]],
                                            axis=0).astype(BF16)


def _mla_proj(st, x, mods, gn, w_in_ext, g_q, g_kv, w_uqt, tabqt):
    n = st.n
    n_sub = TOKEN_TILE // MLA_Q_SUB
    tab_tiles = tabqt.shape[1] // TOKEN_TILE
    return pl.pallas_call(
        _mla_proj_kernel,
        grid=(st.tiles,),
        in_specs=[st.tok(D_MODEL), st.mod(), _resident((2, D_MODEL)), _resident(w_in_ext.shape),
                  _resident((1, MLA_Q_RANK)), _resident((1, MLA_KV_RANK)), _resident(w_uqt.shape),
                  pl.BlockSpec((2 * MLA_ROPE, TOKEN_TILE), lambda i: (0, i % tab_tiles))],
        out_specs=[pl.BlockSpec((n_sub, MLA_HEADS, 256, MLA_Q_SUB), lambda i: (i, 0, 0, 0)),
                   st.tok(MLA_KV_RANK), st.tok(2 * MLA_ROPE)],
        out_shape=[jax.ShapeDtypeStruct((n // MLA_Q_SUB, MLA_HEADS, 256, MLA_Q_SUB), BF16),
                   jax.ShapeDtypeStruct((n, MLA_KV_RANK), F32),
                   jax.ShapeDtypeStruct((n, 2 * MLA_ROPE), F32)],
        compiler_params=_params("parallel"),
        name="mla_proj",
    )(x, mods, gn, w_in_ext, g_q, g_kv, w_uqt, tabqt)


def _mla_build_kv(ckv, kr_dup, wukv, k_dst, vt_dst):
    k_dst[:, MLA_NOPE:] = kr_dup
    for lo, hi in _mla_key_parts(ckv.shape[0]):
        kv = _dot(ckv[lo:hi, :], wukv)
        k_dst[lo:hi, 0:MLA_NOPE] = kv[:, 0:MLA_NOPE].astype(BF16)
        vt_dst[:, lo:hi] = kv[:, MLA_NOPE:].T.astype(BF16)


def _mla_key_parts(tk):
    half = (tk // 2 + 255) // 256 * 256
    return [(0, tk)] if tk <= 256 else [(0, half), (half, tk)]


def _mla_colmax(sts):
    return functools.reduce(jnp.maximum, [jnp.max(s, axis=0, keepdims=True) for s in sts])


def _mla_softmax_pv(sts, m, vt_parts):
    pts = [jnp.exp2(s - m) for s in sts]
    denom = functools.reduce(jnp.add, [jnp.sum(p, axis=0, keepdims=True) for p in pts])
    ot = functools.reduce(jnp.add, [_dot(vt, p.astype(BF16)) for vt, p in zip(vt_parts, pts)])
    return (ot / denom).T.astype(BF16)


def _mla_attn_kernel(qt_ref, ckv_ref, kr_ref, wukv_ref, o_ref, k_scr, vt_scr):
    kr = kr_ref[...]
    kr_dup = (kr + pltpu.roll(kr, MLA_ROPE, axis=1)).astype(BF16)
    ckv = ckv_ref[...]

    def scores(h):
        _mla_build_kv(ckv, kr_dup, wukv_ref[h], k_scr.at[h], vt_scr.at[h])
        return _dot(k_scr[h], qt_ref[h])

    nxt = scores(0)
    for h in range(MLA_HEADS):
        cur, nxt = nxt, (scores(h + 1) if h + 1 < MLA_HEADS else None)
        o_ref[:, h * MLA_V:(h + 1) * MLA_V] = _mla_softmax_pv([cur], _mla_colmax([cur]), [vt_scr[h]])


def _mla_attn(batch, seq, qt, ckv3, kr_masked, w_ukv):
    return pl.pallas_call(
        _mla_attn_kernel,
        grid=(batch,),
        in_specs=[pl.BlockSpec((None, MLA_HEADS, 256, seq), lambda b: (b, 0, 0, 0)),
                  pl.BlockSpec((None, seq, MLA_KV_RANK), lambda b: (b, 0, 0)),
                  pl.BlockSpec((None, seq, 2 * MLA_ROPE), lambda b: (b, 0, 0)),
                  _resident(w_ukv.shape)],
        out_specs=pl.BlockSpec((seq, MLA_HEADS * MLA_V), lambda b: (b, 0)),
        out_shape=jax.ShapeDtypeStruct((batch * seq, MLA_HEADS * MLA_V), BF16),
        scratch_shapes=[pltpu.VMEM((MLA_HEADS, seq, 256), BF16), pltpu.VMEM((MLA_HEADS, MLA_V, seq), BF16)],
        compiler_params=_params("parallel"),
        name="mla_attn",
    )(qt, ckv3, kr_masked, w_ukv)


def _mla_attn_long_kernel(qt_ref, ckv_ref, kr_ref, tabk_ref, wukv_ref, o_ref,
                          k_scr, vt_scr, st_scr, m_scr, acc_scr, den_scr):
    n_sub = qt_ref.shape[0]
    tk = k_scr.shape[0]
    tiles = [(lo, min(lo + MLA_KEY_TILE, tk)) for lo in range(0, tk, MLA_KEY_TILE)]
    halves = [[(lo, (lo + hi) // 2), ((lo + hi) // 2, hi)] for lo, hi in tiles]

    y = kr_ref[...] * tabk_ref[...]
    _mla_build_kv(ckv_ref[...], (y + pltpu.roll(y, MLA_ROPE, axis=1)).astype(BF16), wukv_ref[...], k_scr, vt_scr)

    def fold8(x, op):
        return op(x.reshape(x.shape[0] // 8, 8, x.shape[1]), axis=0)

    def finalize(j, slot):
        start = j * MLA_Q_SUB
        rows = pl.ds(start if isinstance(j, int) else pl.multiple_of(start, MLA_Q_SUB), MLA_Q_SUB)
        o_ref[rows, :] = (acc_scr[slot] / jnp.sum(den_scr[slot], axis=0, keepdims=True)).T.astype(BF16)

    def stage(j_next, slot_next, j_cur, slot_cur, j_done):
        qt = None if j_next is None else qt_ref[j_next]
        if j_cur is not None:
            m = jnp.max(m_scr[slot_cur], axis=0, keepdims=True)
            acc = jnp.zeros((MLA_V, MLA_Q_SUB), F32)
            den = jnp.zeros((8, MLA_Q_SUB), F32)
        run_max = None
        for n, pieces in enumerate(halves):
            for lo, hi in pieces:
                if qt is not None:
                    s = _dot(k_scr[lo:hi, :], qt)
                    st_scr[slot_next, lo:hi, :] = s
                    s8 = fold8(s, jnp.max)
                    run_max = s8 if run_max is None else jnp.maximum(run_max, s8)
            for lo, hi in pieces:
                if j_cur is not None:
                    p = jnp.exp2(st_scr[slot_cur, lo:hi, :] - m)
                    den = den + fold8(p, jnp.sum)
                    acc = acc + _dot(vt_scr[:, lo:hi], p.astype(BF16))
            if n == 0 and j_done is not None:
                finalize(j_done, 1 - slot_cur)
        if qt is not None:
            m_scr[slot_next] = run_max
        if j_cur is not None:
            acc_scr[slot_cur] = acc
            den_scr[slot_cur] = den

    acc_scr[1] = jnp.zeros(acc_scr.shape[1:], F32)
    den_scr[1] = jnp.ones(den_scr.shape[1:], F32)
    stage(0, 0, None, None, None)

    def trip(j):
        for i in range(MLA_UNROLL):
            last = isinstance(j, int) and j + i + 1 == n_sub
            done = max(j + i - 1, 0) if isinstance(j, int) else jnp.maximum(j + i - 1, 0)
            stage(None if last else j + i + 1, (i + 1) % 2, j + i, i % 2, done)

    def body(t, carry):
        trip(MLA_UNROLL * t)
        return carry

    lax.fori_loop(0, n_sub // MLA_UNROLL - 1, body, 0)
    trip(n_sub - MLA_UNROLL)
    finalize(n_sub - 1, (n_sub - 1) % 2)


def _mla_attn_long(batch, tq_total, qt, ckv_all, kr_all, tabk, w_ukv):
    tk = ckv_all.shape[1]
    n_sub = tq_total // MLA_Q_SUB
    once = pl.Buffered(1)
    return pl.pallas_call(
        _mla_attn_long_kernel,
        grid=(batch, MLA_HEADS),
        in_specs=[pl.BlockSpec((n_sub, None, 256, MLA_Q_SUB), lambda b, h: (b, h, 0, 0)),
                  pl.BlockSpec((None, tk, MLA_KV_RANK), lambda b, h: (b, 0, 0), pipeline_mode=once),
                  pl.BlockSpec((None, tk, 2 * MLA_ROPE), lambda b, h: (b, 0, 0), pipeline_mode=once),
                  pl.BlockSpec((tk, 2 * MLA_ROPE), lambda b, h: (0, 0), pipeline_mode=once),
                  pl.BlockSpec((None, MLA_KV_RANK, 256), lambda b, h: (h, 0, 0))],
        out_specs=pl.BlockSpec((tq_total, MLA_V), lambda b, h: (b, h)),
        out_shape=jax.ShapeDtypeStruct((batch * tq_total, MLA_HEADS * MLA_V), BF16),
        scratch_shapes=[pltpu.VMEM((tk, 256), BF16), pltpu.VMEM((MLA_V, tk), BF16),
                        pltpu.VMEM((2, tk, MLA_Q_SUB), F32), pltpu.VMEM((2, 8, MLA_Q_SUB), F32),
                        pltpu.VMEM((2, MLA_V, MLA_Q_SUB), F32), pltpu.VMEM((2, 8, MLA_Q_SUB), F32)],
        compiler_params=_params("parallel", "arbitrary"),
        name="mla_attn_long",
    )(qt, ckv_all, kr_all, tabk, w_ukv)


def _mla_post_kernel(a_ref, x_ref, mod_ref, gn_ref, w_ref, out_ref):
    out_ref[...] = _residual(x_ref[...], _dot(a_ref[...], w_ref[...]), mod_ref, gn_ref, 1, 1.0)


def _mla_post(st, a, x, mods, gn, w_out):
    return pl.pallas_call(
        _mla_post_kernel,
        grid=(st.tiles,),
        in_specs=[st.tok(MLA_HEADS * MLA_V), st.tok(D_MODEL), st.mod(), _resident((2, D_MODEL)),
                  _resident((MLA_HEADS * MLA_V, D_MODEL))],
        out_specs=st.tok(D_MODEL),
        out_shape=jax.ShapeDtypeStruct((st.n, D_MODEL), F32),
        compiler_params=_params("parallel"),
        name="mla_post",
    )(a, x, mods, gn, w_out)


_ROPE_SWAP = tuple(list(range(16, 32)) + list(range(0, 16)) + list(range(48, 64)) + list(range(32, 48)))


def _rope_table(n_tokens):
    rows = n_tokens // GRID_W
    row = np.repeat(np.arange(rows), GRID_W).astype(np.float64)
    col = np.tile(np.arange(GRID_W), rows).astype(np.float64)
    n_pairs = MLA_ROPE // 4
    inv = ROPE_BASE ** (-np.arange(n_pairs, dtype=np.float64) / n_pairs)
    ar, ac = row[:, None] * inv, col[:, None] * inv
    cos = np.concatenate([np.cos(ar), np.cos(ar), np.cos(ac), np.cos(ac)], axis=1)
    sin = np.concatenate([-np.sin(ar), np.sin(ar), -np.sin(ac), np.sin(ac)], axis=1)
    return np.concatenate([cos, sin], axis=1).astype(np.float32)


def _identity_table(n_tokens):
    return np.concatenate([np.ones((n_tokens, MLA_ROPE), np.float32), np.zeros((n_tokens, MLA_ROPE), np.float32)], axis=1)


def kernel(x_prompt, x_sample, state_gla, cache_mla_ckv, cache_mla_krope, c, c_ctx, w_mod, b_mod, g_norm, w_ffn_in, w_ffn_out, gla_w_in, gla_w_gate, gla_b_gate, gla_g_out, gla_w_out, mla_w_in, mla_g_q, mla_g_kv, mla_w_uq, mla_w_ukv, mla_w_out):
    batch, seq = x_prompt.shape[0], x_prompt.shape[1]
    dec_batch, dec_seq = x_sample.shape[0], x_sample.shape[1]
    past = cache_mla_ckv.shape[2]
    ctx = _Stream(batch, seq, dec_batch, per_batch_mod=False)
    lat = _Stream(dec_batch, dec_seq, 0, per_batch_mod=True)

    cond8 = jnp.concatenate([c, c_ctx[None, :], jnp.zeros((8 - dec_batch - 1, D_MODEL), F32)], axis=0)
    mods = _modulation(cond8, w_mod, b_mod)

    xs = {ctx: x_prompt.reshape(ctx.n, D_MODEL), lat: x_sample.reshape(lat.n, D_MODEL)}
    w_ffn_in_b = w_ffn_in.astype(BF16)
    w_ffn_out_b = w_ffn_out.astype(BF16)
    gla_states, mla_ckvs, mla_krs = [], [], []

    for l in range(DEPTH):
        j = l // 2
        for st in (ctx, lat):
            xs[st] = _ffn(st, xs[st], mods[l], g_norm[l, 0], w_ffn_in_b, w_ffn_out_b, l, 0, 0)

        if l % 2 == 0:
            w_in = gla_w_in[j].astype(BF16)
            w_qkvr, w_z = w_in[:, :2 * GLA_QK + 2 * GLA_VV], w_in[:, 2 * GLA_QK + 2 * GLA_VV:]
            zeros = jnp.zeros((GLA_GATE_RANK, GLA_QK), F32)
            w_gate_bd = jnp.concatenate([jnp.concatenate([gla_w_gate[j, 0], zeros], axis=1),
                                         jnp.concatenate([zeros, gla_w_gate[j, 1]], axis=1)], axis=0).astype(BF16)
            b_gate = gla_b_gate[j].reshape(1, 2 * GLA_QK)
            g_out = gla_g_out[j].reshape(1, GLA_VV)
            w_out = gla_w_out[j].astype(BF16)
            for st in (ctx, lat):
                q, k, v, r, la = _gla_proj(st, xs[st], mods[l], g_norm[l, 1], w_qkvr, w_z, w_gate_bd, b_gate)
                if st is ctx:
                    o, s_fin = _gla_scan(st, q, k, v, la, None, True)
                    gla_states.append(s_fin)
                else:
                    (o,) = _gla_scan(st, q, k, v, la, state_gla[:, j].astype(F32), False)
                xs[st] = _gla_post(st, o, r, xs[st], mods[l], g_norm[l, 1], g_out, w_out)
        else:
            swap = jnp.array(_ROPE_SWAP)
            w_in = mla_w_in[j]
            w_in_ext = jnp.concatenate([w_in, w_in[:, MLA_Q_RANK + MLA_KV_RANK:][:, swap]], axis=1).astype(BF16)
            w_uq = mla_w_uq[j].reshape(MLA_Q_RANK, MLA_HEADS, MLA_NOPE + MLA_ROPE)
            w_uq = jnp.concatenate([w_uq, w_uq[:, :, MLA_NOPE:][:, :, swap]], axis=2)
            w_uqt = jnp.transpose(w_uq, (1, 2, 0)).astype(BF16)
            w_ukv = jnp.transpose(mla_w_ukv[j].reshape(MLA_KV_RANK, MLA_HEADS, MLA_NOPE + MLA_V), (1, 0, 2)).astype(BF16)
            g_q = mla_g_q[j].reshape(1, MLA_Q_RANK)
            g_kv = mla_g_kv[j].reshape(1, MLA_KV_RANK)
            w_out = mla_w_out[j].astype(BF16)
            tabq = _rope_table(lat.seq)
            tabqt = {ctx: jnp.asarray(_identity_table(TOKEN_TILE).T), lat: jnp.asarray(tabq.T)}
            for st in (ctx, lat):
                qt, ckv, kr2 = _mla_proj(st, xs[st], mods[l], g_norm[l, 1], w_in_ext, g_q, g_kv, w_uqt, tabqt[st])
                ckv3 = ckv.reshape(st.batch, st.seq, MLA_KV_RANK)
                kr3 = kr2.reshape(st.batch, st.seq, 2 * MLA_ROPE)
                if st is ctx:
                    mla_ckvs.append(ckv3)
                    kr = kr3[:, :, :MLA_ROPE]
                    mla_krs.append(kr)
                    kr_masked = jnp.concatenate([kr, jnp.zeros_like(kr)], axis=2)
                    a = _mla_attn(st.batch, st.seq, qt, ckv3.astype(BF16), kr_masked, w_ukv)
                else:
                    ckv_all = jnp.concatenate([cache_mla_ckv[:, j].astype(F32), ckv3], axis=1).astype(BF16)
                    kr_ctx = cache_mla_krope[:, j].astype(F32)
                    kr_all = jnp.concatenate([jnp.concatenate([kr_ctx, jnp.zeros_like(kr_ctx)], axis=2), kr3], axis=1)
                    tabk = jnp.asarray(np.concatenate([_identity_table(past), tabq], axis=0))
                    a = _mla_attn_long(st.batch, st.seq, qt, ckv_all, kr_all, tabk, w_ukv)
                xs[st] = _mla_post(st, a, xs[st], mods[l], g_norm[l, 1], w_out)

        for st in (ctx, lat):
            xs[st] = _ffn(st, xs[st], mods[l], g_norm[l, 2], w_ffn_in_b, w_ffn_out_b, l, 1, 2)

    yp = xs[ctx].reshape(batch, seq, D_MODEL)
    ys = xs[lat].reshape(dec_batch, dec_seq, D_MODEL)
    new_state_gla = jnp.stack(gla_states, axis=1).astype(x_prompt.dtype)
    new_cache_mla_ckv = jnp.stack(mla_ckvs, axis=1)
    new_cache_mla_krope = jnp.stack(mla_krs, axis=1)
    return (yp, ys, new_state_gla, new_cache_mla_ckv, new_cache_mla_krope)
```

```python
import functools

import jax
import jax.numpy as jnp
import numpy as np
from jax import lax
from jax.experimental import pallas as pl
from jax.experimental.pallas import tpu as pltpu

F32 = jnp.float32
BF16 = jnp.bfloat16

D_MODEL = 1024
DEPTH = 2
GRID_W = 64
D_FF = 2816
MACARON_W = 0.5
N_MOD = 9
EPS = 1e-6
LOG2_E = 1.4426950408889634

GLA_HEADS = 4
GLA_DK = 128
GLA_DV = 256
GLA_GATE_RANK = 16
GLA_TAU = 16.0
GLA_QK = GLA_HEADS * GLA_DK
GLA_VV = GLA_HEADS * GLA_DV

MLA_HEADS = 16
MLA_NOPE = 128
MLA_ROPE = 64
MLA_V = 128
MLA_Q_RANK = 512
MLA_KV_RANK = 256
MLA_SCALE = (MLA_NOPE + MLA_ROPE) ** -0.5
ROPE_BASE = 10000.0

VMEM_LIMIT_BYTES = 56 * 1024 * 1024

TOKEN_TILE = 512
FFN_CHUNK = 1408
GLA_CHUNK = 64
GLA_BLOCK = 512
GLA_SAFE_LOG2 = 120.0
MLA_Q_SUB = 256
MLA_KEY_TILE = 512
MLA_UNROLL = 4


def _params(*sem):
    return pltpu.CompilerParams(dimension_semantics=sem, vmem_limit_bytes=VMEM_LIMIT_BYTES)


def _resident(shape):
    nd = len(shape)
    return pl.BlockSpec(shape, lambda *_: (0,) * nd, pipeline_mode=pl.Buffered(1))


def _silu(x):
    return x * (1.0 / (1.0 + jnp.exp(-x)))


def _rms(x, g):
    return x * lax.rsqrt(jnp.mean(x * x, axis=-1, keepdims=True) + EPS) * g


def _pre(x, mod_ref, gn_ref, s):
    return _rms(x, gn_ref[0:1, :]) * (1.0 + mod_ref[3 * s + 1]) + mod_ref[3 * s]


def _residual(x, y, mod_ref, gn_ref, s, weight):
    return x + (weight * mod_ref[3 * s + 2]) * _rms(y, gn_ref[1:2, :])


def _dot(a, b):
    return jnp.dot(a, b, preferred_element_type=F32)


def _dot_nt(a, b):
    return lax.dot_general(a, b, (((1,), (1,)), ((), ())), preferred_element_type=F32)


def _dot_tn(a, b):
    return lax.dot_general(a, b, (((0,), (0,)), ((), ())), preferred_element_type=F32)


def _mod_kernel(cond_ref, w_ref, b_ref, o_ref):
    a = _silu(cond_ref[...]).astype(BF16)
    o_ref[...] = _dot(a, w_ref[...].astype(BF16)) + b_ref[...]


def _modulation(cond8, w_mod, b_mod):
    out = pl.pallas_call(
        _mod_kernel,
        grid=(DEPTH, N_MOD),
        in_specs=[
            pl.BlockSpec((8, D_MODEL), lambda l, j: (0, 0)),
            pl.BlockSpec((None, D_MODEL, D_MODEL), lambda l, j: (l, 0, j)),
            pl.BlockSpec((None, None, 1, D_MODEL), lambda l, j: (l, j, 0, 0)),
        ],
        out_specs=pl.BlockSpec((None, None, 8, D_MODEL), lambda l, j: (l, j, 0, 0)),
        out_shape=jax.ShapeDtypeStruct((DEPTH, N_MOD, 8, D_MODEL), F32),
        compiler_params=_params("parallel", "parallel"),
        name="modulation",
    )(cond8, w_mod, b_mod.reshape(DEPTH, N_MOD, 1, D_MODEL))
    return out.reshape(DEPTH, N_MOD, 8, 1, D_MODEL)


class _Stream:
    def __init__(self, batch, seq, mod_row0, per_batch_mod):
        self.batch, self.seq = batch, seq
        self.n = batch * seq
        self.tiles = self.n // TOKEN_TILE
        tiles_per_batch = seq // TOKEN_TILE
        if per_batch_mod:
            self.mod_row = lambda i: mod_row0 + i // tiles_per_batch
        else:
            self.mod_row = lambda i: mod_row0

    def tok(self, width):
        return pl.BlockSpec((TOKEN_TILE, width), lambda i: (i, 0))

    def mod(self):
        return pl.BlockSpec((N_MOD, None, 1, D_MODEL), lambda i: (0, self.mod_row(i), 0, 0))


def _swiglu_sublayer(x, mod_ref, gn_ref, win_ref, wout_ref, s):
    h = _pre(x, mod_ref, gn_ref, s).astype(BF16)
    y = jnp.zeros((TOKEN_TILE, D_MODEL), F32)
    for lo in range(0, D_FF, FFN_CHUNK):
        g = _dot(h, win_ref[:, lo:lo + FFN_CHUNK])
        u = _dot(h, win_ref[:, D_FF + lo:D_FF + lo + FFN_CHUNK])
        a = (_silu(g) * u).astype(BF16)
        y = y + _dot(a, wout_ref[lo:lo + FFN_CHUNK, :])
    return _residual(x, y, mod_ref, gn_ref, s, MACARON_W)


def _ffn_specs(layer, which):
    once = pl.Buffered(1)
    return [pl.BlockSpec((None, None, D_MODEL, 2 * D_FF), lambda i: (layer, which, 0, 0), pipeline_mode=once),
            pl.BlockSpec((None, None, D_FF, D_MODEL), lambda i: (layer, which, 0, 0), pipeline_mode=once)]


def _ffn_kernel(x_ref, mod_ref, gn_ref, win_ref, wout_ref, o_ref, *, s):
    o_ref[...] = _swiglu_sublayer(x_ref[...], mod_ref, gn_ref, win_ref, wout_ref, s)


def _ffn(st, x, mods, gn, w_in, w_out, layer, which, s):
    return pl.pallas_call(
        functools.partial(_ffn_kernel, s=s),
        grid=(st.tiles,),
        in_specs=[st.tok(D_MODEL), st.mod(), _resident((2, D_MODEL))] + _ffn_specs(layer, which),
        out_specs=st.tok(D_MODEL),
        out_shape=jax.ShapeDtypeStruct((st.n, D_MODEL), F32),
        compiler_params=_params("parallel"),
        name="ffn",
    )(x, mods, gn, w_in, w_out)


def _gla_proj_kernel(x_ref, mod_ref, gn_ref, w_ref, wz_ref, wg_ref, bg_ref,
                     q_ref, k_ref, v_ref, r_ref, la_ref):
    h = _pre(x_ref[...], mod_ref, gn_ref, 1).astype(BF16)
    z = _dot(h, wz_ref[...]).astype(BF16)
    r_ref[...] = _silu(_dot(h, w_ref[:, 2 * GLA_QK + GLA_VV:2 * GLA_QK + 2 * GLA_VV])).astype(BF16)
    logit = _dot(z, wg_ref[...]) + bg_ref[...]
    la_ref[...] = (jnp.minimum(logit, 0.0) - jnp.log(1.0 + jnp.exp(-jnp.abs(logit)))) * (LOG2_E / GLA_TAU)
    q_ref[...] = _dot(h, w_ref[:, 0:GLA_QK]) * (GLA_DK ** -0.5)
    k_ref[...] = _dot(h, w_ref[:, GLA_QK:2 * GLA_QK])
    v_ref[...] = _dot(h, w_ref[:, 2 * GLA_QK:2 * GLA_QK + GLA_VV]).astype(BF16)


def _gla_proj(st, x, mods, gn, w_qkvr, w_z, w_gate_bd, b_gate):
    n = st.n
    return pl.pallas_call(
        _gla_proj_kernel,
        grid=(st.tiles,),
        in_specs=[st.tok(D_MODEL), st.mod(), _resident((2, D_MODEL)),
                  _resident(w_qkvr.shape), _resident(w_z.shape),
                  _resident(w_gate_bd.shape), _resident(b_gate.shape)],
        out_specs=[st.tok(GLA_QK), st.tok(GLA_QK), st.tok(GLA_VV), st.tok(GLA_VV), st.tok(2 * GLA_QK)],
        out_shape=[jax.ShapeDtypeStruct((n, GLA_QK), F32), jax.ShapeDtypeStruct((n, GLA_QK), F32),
                   jax.ShapeDtypeStruct((n, GLA_VV), BF16), jax.ShapeDtypeStruct((n, GLA_VV), BF16),
                   jax.ShapeDtypeStruct((n, 2 * GLA_QK), F32)],
        compiler_params=_params("parallel"),
        name="gla_proj",
    )(x, mods, gn, w_qkvr, w_z, w_gate_bd, b_gate)


def _gla_scan_kernel(*refs, has_state, want_state):
    it = iter(refs)
    q_ref, k_ref, v_ref, la_ref = next(it), next(it), next(it), next(it)
    s0_ref = next(it) if has_state else None
    o_ref = next(it)
    sout_ref = next(it) if want_state else None
    st_ref = next(it)

    d = pl.program_id(1)
    blk = pl.program_id(2)
    n_chunks = q_ref.shape[0] // GLA_CHUNK
    head_k = [slice(h * GLA_DK, (h + 1) * GLA_DK) for h in range(GLA_HEADS)]
    head_v = [slice(h * GLA_DV, (h + 1) * GLA_DV) for h in range(GLA_HEADS)]

    @pl.when(blk == 0)
    def _():
        if has_state:
            st_ref[...] = s0_ref[...]
        else:
            st_ref[...] = jnp.zeros(st_ref.shape, F32)

    ii = lax.broadcasted_iota(jnp.int32, (GLA_CHUNK, GLA_CHUNK), 0)
    jj = lax.broadcasted_iota(jnp.int32, (GLA_CHUNK, GLA_CHUNK), 1)

    n_groups = GLA_CHUNK // 8
    sub = lax.broadcasted_iota(jnp.int32, (n_groups, 8, GLA_QK), 1)

    def cumsum(x, backward):
        g = x.reshape(n_groups, 8, GLA_QK)
        for step in (1, 2, 4):
            if backward:
                g = g + jnp.where(sub < 8 - step, pltpu.roll(g, 8 - step, axis=1), 0.0)
            else:
                g = g + jnp.where(sub >= step, pltpu.roll(g, step, axis=1), 0.0)
        edge = 0 if backward else 7
        order = range(n_groups - 1, -1, -1) if backward else range(n_groups)
        out, carry = [None] * n_groups, None
        for i in order:
            out[i] = g[i] if carry is None else g[i] + carry
            carry = jnp.broadcast_to(out[i][edge:edge + 1, :], (8, GLA_QK))
        return jnp.concatenate(out, axis=0)

    def pairwise_scores(q, k, b, mask):
        lane = lax.broadcasted_iota(jnp.int32, (GLA_CHUNK, GLA_CHUNK), 1)
        rowi = lax.broadcasted_iota(jnp.int32, (GLA_CHUNK, GLA_DK), 0)
        out = []
        for sk in head_k:
            def key_row(j, att, sk=sk):
                pick = rowi == j
                bj = jnp.sum(jnp.where(pick, b[:, sk], 0.0), axis=0, keepdims=True)
                kj = jnp.sum(jnp.where(pick, k[:, sk], 0.0), axis=0, keepdims=True)
                col = jnp.sum(q[:, sk] * jnp.exp2(jnp.minimum(b[:, sk] - bj, 0.0)) * kj, axis=1, keepdims=True)
                return att + jnp.where(lane == j, col, 0.0)
            att = lax.fori_loop(0, GLA_CHUNK, key_row, jnp.zeros((GLA_CHUNK, GLA_CHUNK), F32))
            out.append(jnp.where(mask, att, 0.0).astype(BF16))
        return out

    def run(backward, factored):
        mask = (jj >= ii) if backward else (jj <= ii)
        order = list(range(n_chunks - 1, -1, -1) if backward else range(n_chunks))

        def prep(c):
            rows = slice(c * GLA_CHUNK, (c + 1) * GLA_CHUNK)
            la = la_ref[rows, :]
            b = cumsum(la, backward)
            btot = jnp.sum(la, axis=0, keepdims=True)
            q = q_ref[rows, :]
            k = k_ref[rows, :]
            if factored:
                bm = b[GLA_CHUNK // 2:GLA_CHUNK // 2 + 1, :]
                qm = (q * jnp.exp2(b - bm)).astype(BF16)
                km = (k * jnp.exp2(bm - b)).astype(BF16)
                att = [jnp.where(mask, _dot_nt(qm[:, sk], km[:, sk]), 0.0).astype(BF16) for sk in head_k]
            else:
                att = pairwise_scores(q, k, b, mask)
            return dict(rows=rows, att=att, v=v_ref[rows, :], qb=(q * jnp.exp2(b)).astype(BF16),
                        kt=(k * jnp.exp2(btot - b)).astype(BF16), dec=jnp.exp2(btot))

        def update(p):
            for h, (sk, sv) in enumerate(zip(head_k, head_v)):
                s = st_ref[h]
                v = p["v"][:, sv]
                o_ref[p["rows"], sv] = (_dot(p["att"][h], v) + _dot(p["qb"][:, sk], s.astype(BF16))).astype(BF16)
                dec_t = jnp.broadcast_to(p["dec"][:, sk], (GLA_DK, GLA_DK)).T
                st_ref[h] = s * jnp.concatenate([dec_t] * (GLA_DV // GLA_DK), axis=1) + _dot_tn(p["kt"][:, sk], v)

        nxt = prep(order[0])
        for n in range(n_chunks):
            cur, nxt = nxt, (prep(order[n + 1]) if n + 1 < n_chunks else None)
            update(cur)

    totals = jnp.sum(la_ref[...].reshape(n_chunks, GLA_CHUNK, GLA_QK), axis=1)
    safe = jnp.min(totals) > -GLA_SAFE_LOG2
    for backward in (False, True):
        for factored in (True, False):
            @pl.when(jnp.logical_and(d == int(backward), safe if factored else jnp.logical_not(safe)))
            def _(backward=backward, factored=factored):
                run(backward, factored)

    if want_state:
        @pl.when(blk == pl.num_programs(2) - 1)
        def _():
            sout_ref[...] = st_ref[...]


def _gla_scan(st, q, k, v, la, s0, want_state):
    block = min(GLA_BLOCK, st.seq)
    nb = st.seq // block
    has_state = s0 is not None

    def row(b, d, i):
        return b * nb + i + d * (nb - 1 - 2 * i)

    state_spec = pl.BlockSpec((None, None, GLA_HEADS, GLA_DK, GLA_DV), lambda b, d, i: (b, d, 0, 0, 0))
    in_specs = [pl.BlockSpec((block, GLA_QK), lambda b, d, i: (row(b, d, i), 0)),
                pl.BlockSpec((block, GLA_QK), lambda b, d, i: (row(b, d, i), 0)),
                pl.BlockSpec((block, GLA_VV), lambda b, d, i: (row(b, d, i), 0)),
                pl.BlockSpec((block, GLA_QK), lambda b, d, i: (row(b, d, i), d))]
    args = [q, k, v, la]
    if has_state:
        in_specs.append(state_spec)
        args.append(s0)
    out_specs = [pl.BlockSpec((None, block, GLA_VV), lambda b, d, i: (d, row(b, d, i), 0))]
    out_shape = [jax.ShapeDtypeStruct((2, st.n, GLA_VV), BF16)]
    if want_state:
        out_specs.append(state_spec)
        out_shape.append(jax.ShapeDtypeStruct((st.batch, 2, GLA_HEADS, GLA_DK, GLA_DV), F32))
    return pl.pallas_call(
        functools.partial(_gla_scan_kernel, has_state=has_state, want_state=want_state),
        grid=(st.batch, 2, nb),
        in_specs=in_specs,
        out_specs=out_specs,
        out_shape=out_shape,
        scratch_shapes=[pltpu.VMEM((GLA_HEADS, GLA_DK, GLA_DV), F32)],
        compiler_params=_params("parallel", "parallel", "arbitrary"),
        name="gla_scan",
    )(*args)


def _gla_post_ffn_kernel(o_ref, r_ref, x_ref, mod_ref, gn_ref, go_ref, w_ref, gn2_ref, win_ref, wout_ref, out_ref):
    o = o_ref[0].astype(F32) + o_ref[1].astype(F32)
    heads = []
    for h in range(GLA_HEADS):
        sv = slice(h * GLA_DV, (h + 1) * GLA_DV)
        heads.append(_rms(o[:, sv], go_ref[:, sv]))
    g = (jnp.concatenate(heads, axis=1) * r_ref[...].astype(F32)).astype(BF16)
    x = _residual(x_ref[...], _dot(g, w_ref[...]), mod_ref, gn_ref, 1, 1.0)
    out_ref[...] = _swiglu_sublayer(x, mod_ref, gn2_ref, win_ref, wout_ref, 2)


def _gla_post_ffn(st, o, r, x, mods, gn, g_out, w_out, gn2, w_ffn_in, w_ffn_out, layer):
    return pl.pallas_call(
        _gla_post_ffn_kernel,
        grid=(st.tiles,),
        in_specs=[pl.BlockSpec((2, TOKEN_TILE, GLA_VV), lambda i: (0, i, 0)),
                  st.tok(GLA_VV), st.tok(D_MODEL), st.mod(), _resident((2, D_MODEL)),
                  _resident((1, GLA_VV)), _resident((GLA_VV, D_MODEL)), _resident((2, D_MODEL))]
                 + _ffn_specs(layer, 1),
        out_specs=st.tok(D_MODEL),
        out_shape=jax.ShapeDtypeStruct((st.n, D_MODEL), F32),
        compiler_params=_params("parallel"),
        name="gla_post_ffn",
    )(o, r, x, mods, gn, g_out, w_out, gn2, w_ffn_in, w_ffn_out)


MLA_Q_SCALE = MLA_SCALE * LOG2_E


def _mla_proj_kernel(x_ref, mod_ref, gn_ref, w_ref, gq_ref, gkv_ref, wuqt_ref, tabqt_ref,
                     qt_ref, ckv_ref, kr_ref):
    h = _pre(x_ref[...], mod_ref, gn_ref, 1).astype(BF16)
    p = _dot(h, w_ref[...])
    ckv_ref[...] = _rms(p[:, MLA_Q_RANK:MLA_Q_RANK + MLA_KV_RANK], gkv_ref[...])
    kr_ref[...] = p[:, MLA_Q_RANK + MLA_KV_RANK:]
    cqt = _rms(p[:, 0:MLA_Q_RANK], gq_ref[...]).T.astype(BF16)
    tab = tabqt_ref[...] * MLA_Q_SCALE
    for hd in range(MLA_HEADS):
        qt = _dot(wuqt_ref[hd], cqt)
        for j in range(TOKEN_TILE // MLA_Q_SUB):
            cols = slice(j * MLA_Q_SUB, (j + 1) * MLA_Q_SUB)
            qt_ref[j, hd] = jnp.concatenate([qt[0:MLA_NOPE, cols] * MLA_Q_SCALE, qt[MLA_NOPE:, cols] * tab[:, cols]],
                                            axis=0).astype(BF16)


def _mla_proj(st, x, mods, gn, w_in_ext, g_q, g_kv, w_uqt, tabqt):
    n = st.n
    n_sub = TOKEN_TILE // MLA_Q_SUB
    tab_tiles = tabqt.shape[1] // TOKEN_TILE
    return pl.pallas_call(
        _mla_proj_kernel,
        grid=(st.tiles,),
        in_specs=[st.tok(D_MODEL), st.mod(), _resident((2, D_MODEL)), _resident(w_in_ext.shape),
                  _resident((1, MLA_Q_RANK)), _resident((1, MLA_KV_RANK)), _resident(w_uqt.shape),
                  pl.BlockSpec((2 * MLA_ROPE, TOKEN_TILE), lambda i: (0, i % tab_tiles))],
        out_specs=[pl.BlockSpec((n_sub, MLA_HEADS, 256, MLA_Q_SUB), lambda i: (i, 0, 0, 0)),
                   st.tok(MLA_KV_RANK), st.tok(2 * MLA_ROPE)],
        out_shape=[jax.ShapeDtypeStruct((n // MLA_Q_SUB, MLA_HEADS, 256, MLA_Q_SUB), BF16),
                   jax.ShapeDtypeStruct((n, MLA_KV_RANK), F32),
                   jax.ShapeDtypeStruct((n, 2 * MLA_ROPE), F32)],
        compiler_params=_params("parallel"),
        name="mla_proj",
    )(x, mods, gn, w_in_ext, g_q, g_kv, w_uqt, tabqt)


def _mla_build_kv(ckv, kr_dup, wukv, k_dst, vt_dst):
    k_dst[:, MLA_NOPE:] = kr_dup
    for lo, hi in _mla_key_parts(ckv.shape[0]):
        kv = _dot(ckv[lo:hi, :], wukv)
        k_dst[lo:hi, 0:MLA_NOPE] = kv[:, 0:MLA_NOPE].astype(BF16)
        vt_dst[:, lo:hi] = kv[:, MLA_NOPE:].T.astype(BF16)


def _mla_key_parts(tk):
    half = (tk // 2 + 255) // 256 * 256
    return [(0, tk)] if tk <= 256 else [(0, half), (half, tk)]


def _mla_colmax(sts):
    return functools.reduce(jnp.maximum, [jnp.max(s, axis=0, keepdims=True) for s in sts])


def _mla_softmax_pv(sts, m, vt_parts):
    pts = [jnp.exp2(s - m) for s in sts]
    denom = functools.reduce(jnp.add, [jnp.sum(p, axis=0, keepdims=True) for p in pts])
    ot = functools.reduce(jnp.add, [_dot(vt, p.astype(BF16)) for vt, p in zip(vt_parts, pts)])
    return (ot / denom).T.astype(BF16)


def _mla_attn_kernel(qt_ref, ckv_ref, kr_ref, wukv_ref, o_ref, k_scr, vt_scr):
    kr = kr_ref[...]
    kr_dup = (kr + pltpu.roll(kr, MLA_ROPE, axis=1)).astype(BF16)
    ckv = ckv_ref[...]

    def scores(h):
        _mla_build_kv(ckv, kr_dup, wukv_ref[h], k_scr.at[h], vt_scr.at[h])
        return _dot(k_scr[h], qt_ref[h])

    nxt = scores(0)
    for h in range(MLA_HEADS):
        cur, nxt = nxt, (scores(h + 1) if h + 1 < MLA_HEADS else None)
        o_ref[:, h * MLA_V:(h + 1) * MLA_V] = _mla_softmax_pv([cur], _mla_colmax([cur]), [vt_scr[h]])


def _mla_attn(batch, seq, qt, ckv3, kr_masked, w_ukv):
    return pl.pallas_call(
        _mla_attn_kernel,
        grid=(batch,),
        in_specs=[pl.BlockSpec((None, MLA_HEADS, 256, seq), lambda b: (b, 0, 0, 0)),
                  pl.BlockSpec((None, seq, MLA_KV_RANK), lambda b: (b, 0, 0)),
                  pl.BlockSpec((None, seq, 2 * MLA_ROPE), lambda b: (b, 0, 0)),
                  _resident(w_ukv.shape)],
        out_specs=pl.BlockSpec((seq, MLA_HEADS * MLA_V), lambda b: (b, 0)),
        out_shape=jax.ShapeDtypeStruct((batch * seq, MLA_HEADS * MLA_V), BF16),
        scratch_shapes=[pltpu.VMEM((MLA_HEADS, seq, 256), BF16), pltpu.VMEM((MLA_HEADS, MLA_V, seq), BF16)],
        compiler_params=_params("parallel"),
        name="mla_attn",
    )(qt, ckv3, kr_masked, w_ukv)


def _mla_attn_long_kernel(qt_ref, ckv_ref, kr_ref, tabk_ref, wukv_ref, o_ref,
                          k_scr, vt_scr, st_scr, m_scr, acc_scr, den_scr):
    n_sub = qt_ref.shape[0]
    tk = k_scr.shape[0]
    tiles = [(lo, min(lo + MLA_KEY_TILE, tk)) for lo in range(0, tk, MLA_KEY_TILE)]
    halves = [[(lo, (lo + hi) // 2), ((lo + hi) // 2, hi)] for lo, hi in tiles]

    y = kr_ref[...] * tabk_ref[...]
    _mla_build_kv(ckv_ref[...], (y + pltpu.roll(y, MLA_ROPE, axis=1)).astype(BF16), wukv_ref[...], k_scr, vt_scr)

    def fold8(x, op):
        return op(x.reshape(x.shape[0] // 8, 8, x.shape[1]), axis=0)

    def finalize(j, slot):
        start = j * MLA_Q_SUB
        rows = pl.ds(start if isinstance(j, int) else pl.multiple_of(start, MLA_Q_SUB), MLA_Q_SUB)
        o_ref[rows, :] = (acc_scr[slot] / jnp.sum(den_scr[slot], axis=0, keepdims=True)).T.astype(BF16)

    def stage(j_next, slot_next, j_cur, slot_cur, j_done):
        qt = None if j_next is None else qt_ref[j_next]
        if j_cur is not None:
            m = jnp.max(m_scr[slot_cur], axis=0, keepdims=True)
            acc = jnp.zeros((MLA_V, MLA_Q_SUB), F32)
            den = jnp.zeros((8, MLA_Q_SUB), F32)
        run_max = None
        for n, pieces in enumerate(halves):
            for lo, hi in pieces:
                if qt is not None:
                    s = _dot(k_scr[lo:hi, :], qt)
                    st_scr[slot_next, lo:hi, :] = s
                    s8 = fold8(s, jnp.max)
                    run_max = s8 if run_max is None else jnp.maximum(run_max, s8)
            for lo, hi in pieces:
                if j_cur is not None:
                    p = jnp.exp2(st_scr[slot_cur, lo:hi, :] - m)
                    den = den + fold8(p, jnp.sum)
                    acc = acc + _dot(vt_scr[:, lo:hi], p.astype(BF16))
            if n == 0 and j_done is not None:
                finalize(j_done, 1 - slot_cur)
        if qt is not None:
            m_scr[slot_next] = run_max
        if j_cur is not None:
            acc_scr[slot_cur] = acc
            den_scr[slot_cur] = den

    acc_scr[1] = jnp.zeros(acc_scr.shape[1:], F32)
    den_scr[1] = jnp.ones(den_scr.shape[1:], F32)
    stage(0, 0, None, None, None)

    def trip(j):
        for i in range(MLA_UNROLL):
            last = isinstance(j, int) and j + i + 1 == n_sub
            done = max(j + i - 1, 0) if isinstance(j, int) else jnp.maximum(j + i - 1, 0)
            stage(None if last else j + i + 1, (i + 1) % 2, j + i, i % 2, done)

    def body(t, carry):
        trip(MLA_UNROLL * t)
        return carry

    lax.fori_loop(0, n_sub // MLA_UNROLL - 1, body, 0)
    trip(n_sub - MLA_UNROLL)
    finalize(n_sub - 1, (n_sub - 1) % 2)


def _mla_attn_long(batch, tq_total, qt, ckv_all, kr_all, tabk, w_ukv):
    tk = ckv_all.shape[1]
    n_sub = tq_total // MLA_Q_SUB
    once = pl.Buffered(1)
    return pl.pallas_call(
        _mla_attn_long_kernel,
        grid=(batch, MLA_HEADS),
        in_specs=[pl.BlockSpec((n_sub, None, 256, MLA_Q_SUB), lambda b, h: (b, h, 0, 0)),
                  pl.BlockSpec((None, tk, MLA_KV_RANK), lambda b, h: (b, 0, 0), pipeline_mode=once),
                  pl.BlockSpec((None, tk, 2 * MLA_ROPE), lambda b, h: (b, 0, 0), pipeline_mode=once),
                  pl.BlockSpec((tk, 2 * MLA_ROPE), lambda b, h: (0, 0), pipeline_mode=once),
                  pl.BlockSpec((None, MLA_KV_RANK, 256), lambda b, h: (h, 0, 0))],
        out_specs=pl.BlockSpec((tq_total, MLA_V), lambda b, h: (b, h)),
        out_shape=jax.ShapeDtypeStruct((batch * tq_total, MLA_HEADS * MLA_V), BF16),
        scratch_shapes=[pltpu.VMEM((tk, 256), BF16), pltpu.VMEM((MLA_V, tk), BF16),
                        pltpu.VMEM((2, tk, MLA_Q_SUB), F32), pltpu.VMEM((2, 8, MLA_Q_SUB), F32),
                        pltpu.VMEM((2, MLA_V, MLA_Q_SUB), F32), pltpu.VMEM((2, 8, MLA_Q_SUB), F32)],
        compiler_params=_params("parallel", "arbitrary"),
        name="mla_attn_long",
    )(qt, ckv_all, kr_all, tabk, w_ukv)


def _mla_post_ffn_kernel(a_ref, x_ref, mod_ref, gn_ref, w_ref, gn2_ref, win_ref, wout_ref, out_ref):
    x = _residual(x_ref[...], _dot(a_ref[...], w_ref[...]), mod_ref, gn_ref, 1, 1.0)
    out_ref[...] = _swiglu_sublayer(x, mod_ref, gn2_ref, win_ref, wout_ref, 2)


def _mla_post_ffn(st, a, x, mods, gn, w_out, gn2, w_ffn_in, w_ffn_out, layer):
    return pl.pallas_call(
        _mla_post_ffn_kernel,
        grid=(st.tiles,),
        in_specs=[st.tok(MLA_HEADS * MLA_V), st.tok(D_MODEL), st.mod(), _resident((2, D_MODEL)),
                  _resident((MLA_HEADS * MLA_V, D_MODEL)), _resident((2, D_MODEL))] + _ffn_specs(layer, 1),
        out_specs=st.tok(D_MODEL),
        out_shape=jax.ShapeDtypeStruct((st.n, D_MODEL), F32),
        compiler_params=_params("parallel"),
        name="mla_post_ffn",
    )(a, x, mods, gn, w_out, gn2, w_ffn_in, w_ffn_out)


_ROPE_SWAP = tuple(list(range(16, 32)) + list(range(0, 16)) + list(range(48, 64)) + list(range(32, 48)))


def _rope_table(n_tokens):
    rows = n_tokens // GRID_W
    row = np.repeat(np.arange(rows), GRID_W).astype(np.float64)
    col = np.tile(np.arange(GRID_W), rows).astype(np.float64)
    n_pairs = MLA_ROPE // 4
    inv = ROPE_BASE ** (-np.arange(n_pairs, dtype=np.float64) / n_pairs)
    ar, ac = row[:, None] * inv, col[:, None] * inv
    cos = np.concatenate([np.cos(ar), np.cos(ar), np.cos(ac), np.cos(ac)], axis=1)
    sin = np.concatenate([-np.sin(ar), np.sin(ar), -np.sin(ac), np.sin(ac)], axis=1)
    return np.concatenate([cos, sin], axis=1).astype(np.float32)


def _identity_table(n_tokens):
    return np.concatenate([np.ones((n_tokens, MLA_ROPE), np.float32), np.zeros((n_tokens, MLA_ROPE), np.float32)], axis=1)


def kernel(x_prompt, x_sample, state_gla, cache_mla_ckv, cache_mla_krope, c, c_ctx, w_mod, b_mod, g_norm, w_ffn_in, w_ffn_out, gla_w_in, gla_w_gate, gla_b_gate, gla_g_out, gla_w_out, mla_w_in, mla_g_q, mla_g_kv, mla_w_uq, mla_w_ukv, mla_w_out):
    batch, seq = x_prompt.shape[0], x_prompt.shape[1]
    dec_batch, dec_seq = x_sample.shape[0], x_sample.shape[1]
    past = cache_mla_ckv.shape[2]
    ctx = _Stream(batch, seq, dec_batch, per_batch_mod=False)
    lat = _Stream(dec_batch, dec_seq, 0, per_batch_mod=True)

    cond8 = jnp.concatenate([c, c_ctx[None, :], jnp.zeros((8 - dec_batch - 1, D_MODEL), F32)], axis=0)
    mods = _modulation(cond8, w_mod, b_mod)

    xs = {ctx: x_prompt.reshape(ctx.n, D_MODEL), lat: x_sample.reshape(lat.n, D_MODEL)}
    w_ffn_in_b = w_ffn_in.astype(BF16)
    w_ffn_out_b = w_ffn_out.astype(BF16)
    gla_states, mla_ckvs, mla_krs = [], [], []

    for l in range(DEPTH):
        j = l // 2
        for st in (ctx, lat):
            xs[st] = _ffn(st, xs[st], mods[l], g_norm[l, 0], w_ffn_in_b, w_ffn_out_b, l, 0, 0)

        if l % 2 == 0:
            w_in = gla_w_in[j].astype(BF16)
            w_qkvr, w_z = w_in[:, :2 * GLA_QK + 2 * GLA_VV], w_in[:, 2 * GLA_QK + 2 * GLA_VV:]
            zeros = jnp.zeros((GLA_GATE_RANK, GLA_QK), F32)
            w_gate_bd = jnp.concatenate([jnp.concatenate([gla_w_gate[j, 0], zeros], axis=1),
                                         jnp.concatenate([zeros, gla_w_gate[j, 1]], axis=1)], axis=0).astype(BF16)
            b_gate = gla_b_gate[j].reshape(1, 2 * GLA_QK)
            g_out = gla_g_out[j].reshape(1, GLA_VV)
            w_out = gla_w_out[j].astype(BF16)
            for st in (ctx, lat):
                q, k, v, r, la = _gla_proj(st, xs[st], mods[l], g_norm[l, 1], w_qkvr, w_z, w_gate_bd, b_gate)
                if st is ctx:
                    o, s_fin = _gla_scan(st, q, k, v, la, None, True)
                    gla_states.append(s_fin)
                else:
                    (o,) = _gla_scan(st, q, k, v, la, state_gla[:, j].astype(F32), False)
                xs[st] = _gla_post_ffn(st, o, r, xs[st], mods[l], g_norm[l, 1], g_out, w_out,
                                       g_norm[l, 2], w_ffn_in_b, w_ffn_out_b, l)
        else:
            swap = jnp.array(_ROPE_SWAP)
            w_in = mla_w_in[j]
            w_in_ext = jnp.concatenate([w_in, w_in[:, MLA_Q_RANK + MLA_KV_RANK:][:, swap]], axis=1).astype(BF16)
            w_uq = mla_w_uq[j].reshape(MLA_Q_RANK, MLA_HEADS, MLA_NOPE + MLA_ROPE)
            w_uq = jnp.concatenate([w_uq, w_uq[:, :, MLA_NOPE:][:, :, swap]], axis=2)
            w_uqt = jnp.transpose(w_uq, (1, 2, 0)).astype(BF16)
            w_ukv = jnp.transpose(mla_w_ukv[j].reshape(MLA_KV_RANK, MLA_HEADS, MLA_NOPE + MLA_V), (1, 0, 2)).astype(BF16)
            g_q = mla_g_q[j].reshape(1, MLA_Q_RANK)
            g_kv = mla_g_kv[j].reshape(1, MLA_KV_RANK)
            w_out = mla_w_out[j].astype(BF16)
            tabq = _rope_table(lat.seq)
            tabqt = {ctx: jnp.asarray(_identity_table(TOKEN_TILE).T), lat: jnp.asarray(tabq.T)}
            for st in (ctx, lat):
                qt, ckv, kr2 = _mla_proj(st, xs[st], mods[l], g_norm[l, 1], w_in_ext, g_q, g_kv, w_uqt, tabqt[st])
                ckv3 = ckv.reshape(st.batch, st.seq, MLA_KV_RANK)
                kr3 = kr2.reshape(st.batch, st.seq, 2 * MLA_ROPE)
                if st is ctx:
                    mla_ckvs.append(ckv3)
                    kr = kr3[:, :, :MLA_ROPE]
                    mla_krs.append(kr)
                    kr_masked = jnp.concatenate([kr, jnp.zeros_like(kr)], axis=2)
                    a = _mla_attn(st.batch, st.seq, qt, ckv3.astype(BF16), kr_masked, w_ukv)
                else:
                    ckv_all = jnp.concatenate([cache_mla_ckv[:, j].astype(F32), ckv3], axis=1).astype(BF16)
                    kr_ctx = cache_mla_krope[:, j].astype(F32)
                    kr_all = jnp.concatenate([jnp.concatenate([kr_ctx, jnp.zeros_like(kr_ctx)], axis=2), kr3], axis=1)
                    tabk = jnp.asarray(np.concatenate([_identity_table(past), tabq], axis=0))
                    a = _mla_attn_long(st.batch, st.seq, qt, ckv_all, kr_all, tabk, w_ukv)
                xs[st] = _mla_post_ffn(st, a, xs[st], mods[l], g_norm[l, 1], w_out,
                                       g_norm[l, 2], w_ffn_in_b, w_ffn_out_b, l)

    yp = xs[ctx].reshape(batch, seq, D_MODEL)
    ys = xs[lat].reshape(dec_batch, dec_seq, D_MODEL)
    new_state_gla = jnp.stack(gla_states, axis=1).astype(x_prompt.dtype)
    new_cache_mla_ckv = jnp.stack(mla_ckvs, axis=1)
    new_cache_mla_krope = jnp.stack(mla_krs, axis=1)
    return (yp, ys, new_state_gla, new_cache_mla_ckv, new_cache_mla_krope)
```

```python
import functools

import jax
import jax.numpy as jnp
import numpy as np
from jax import lax
from jax.experimental import pallas as pl
from jax.experimental.pallas import tpu as pltpu

F32 = jnp.float32
BF16 = jnp.bfloat16

D_MODEL = 1024
DEPTH = 2
GRID_W = 64
D_FF = 2816
MACARON_W = 0.5
N_MOD = 9
EPS = 1e-6
LOG2_E = 1.4426950408889634

GLA_HEADS = 4
GLA_DK = 128
GLA_DV = 256
GLA_GATE_RANK = 16
GLA_TAU = 16.0
GLA_QK = GLA_HEADS * GLA_DK
GLA_VV = GLA_HEADS * GLA_DV

MLA_HEADS = 16
MLA_NOPE = 128
MLA_ROPE = 64
MLA_V = 128
MLA_Q_RANK = 512
MLA_KV_RANK = 256
MLA_SCALE = (MLA_NOPE + MLA_ROPE) ** -0.5
ROPE_BASE = 10000.0

VMEM_LIMIT_BYTES = 56 * 1024 * 1024

TOKEN_TILE = 512
FFN_CHUNK = 2816
GLA_CHUNK = 64
GLA_BLOCK = 512
GLA_SAFE_LOG2 = 120.0
MLA_Q_SUB = 256
MLA_KEY_TILE = 512
MLA_UNROLL = 4


def _params(*sem):
    return pltpu.CompilerParams(dimension_semantics=sem, vmem_limit_bytes=VMEM_LIMIT_BYTES)


def _resident(shape):
    nd = len(shape)
    return pl.BlockSpec(shape, lambda *_: (0,) * nd, pipeline_mode=pl.Buffered(1))


def _silu(x):
    return x * (1.0 / (1.0 + jnp.exp(-x)))


def _rms(x, g):
    return x * lax.rsqrt(jnp.mean(x * x, axis=-1, keepdims=True) + EPS) * g


def _pre(x, mod_ref, gn_ref, s):
    return _rms(x, gn_ref[0:1, :]) * (1.0 + mod_ref[3 * s + 1]) + mod_ref[3 * s]


def _residual(x, y, mod_ref, gn_ref, s, weight):
    return x + (weight * mod_ref[3 * s + 2]) * _rms(y, gn_ref[1:2, :])


def _dot(a, b):
    return jnp.dot(a, b, preferred_element_type=F32)


def _dot_nt(a, b):
    return lax.dot_general(a, b, (((1,), (1,)), ((), ())), preferred_element_type=F32)


def _dot_tn(a, b):
    return lax.dot_general(a, b, (((0,), (0,)), ((), ())), preferred_element_type=F32)


def _mod_kernel(cond_ref, w_ref, b_ref, o_ref):
    a = _silu(cond_ref[...]).astype(BF16)
    o_ref[...] = _dot(a, w_ref[...].astype(BF16)) + b_ref[...]


def _modulation(cond8, w_mod, b_mod):
    out = pl.pallas_call(
        _mod_kernel,
        grid=(DEPTH, N_MOD),
        in_specs=[
            pl.BlockSpec((8, D_MODEL), lambda l, j: (0, 0)),
            pl.BlockSpec((None, D_MODEL, D_MODEL), lambda l, j: (l, 0, j)),
            pl.BlockSpec((None, None, 1, D_MODEL), lambda l, j: (l, j, 0, 0)),
        ],
        out_specs=pl.BlockSpec((None, None, 8, D_MODEL), lambda l, j: (l, j, 0, 0)),
        out_shape=jax.ShapeDtypeStruct((DEPTH, N_MOD, 8, D_MODEL), F32),
        compiler_params=_params("parallel", "parallel"),
        name="modulation",
    )(cond8, w_mod, b_mod.reshape(DEPTH, N_MOD, 1, D_MODEL))
    return out.reshape(DEPTH, N_MOD, 8, 1, D_MODEL)


class _Stream:
    def __init__(self, batch, seq, mod_row0, per_batch_mod):
        self.batch, self.seq = batch, seq
        self.n = batch * seq
        self.tiles = self.n // TOKEN_TILE
        tiles_per_batch = seq // TOKEN_TILE
        if per_batch_mod:
            self.mod_row = lambda i: mod_row0 + i // tiles_per_batch
        else:
            self.mod_row = lambda i: mod_row0

    def tok(self, width):
        return pl.BlockSpec((TOKEN_TILE, width), lambda i: (i, 0))

    def mod(self):
        return pl.BlockSpec((N_MOD, None, 1, D_MODEL), lambda i: (0, self.mod_row(i), 0, 0))


def _swiglu_sublayer(x, mod_ref, gn_ref, win_ref, wout_ref, s):
    h = _pre(x, mod_ref, gn_ref, s).astype(BF16)
    y = jnp.zeros((TOKEN_TILE, D_MODEL), F32)
    for lo in range(0, D_FF, FFN_CHUNK):
        g = _dot(h, win_ref[:, lo:lo + FFN_CHUNK])
        u = _dot(h, win_ref[:, D_FF + lo:D_FF + lo + FFN_CHUNK])
        a = (_silu(g) * u).astype(BF16)
        y = y + _dot(a, wout_ref[lo:lo + FFN_CHUNK, :])
    return _residual(x, y, mod_ref, gn_ref, s, MACARON_W)


def _ffn_specs(layer, which):
    once = pl.Buffered(1)
    return [pl.BlockSpec((None, None, D_MODEL, 2 * D_FF), lambda i: (layer, which, 0, 0), pipeline_mode=once),
            pl.BlockSpec((None, None, D_FF, D_MODEL), lambda i: (layer, which, 0, 0), pipeline_mode=once)]


def _ffn_kernel(x_ref, mod_ref, gn_ref, win_ref, wout_ref, o_ref, *, s):
    o_ref[...] = _swiglu_sublayer(x_ref[...], mod_ref, gn_ref, win_ref, wout_ref, s)


def _ffn(st, x, mods, gn, w_in, w_out, layer, which, s):
    return pl.pallas_call(
        functools.partial(_ffn_kernel, s=s),
        grid=(st.tiles,),
        in_specs=[st.tok(D_MODEL), st.mod(), _resident((2, D_MODEL))] + _ffn_specs(layer, which),
        out_specs=st.tok(D_MODEL),
        out_shape=jax.ShapeDtypeStruct((st.n, D_MODEL), F32),
        compiler_params=_params("parallel"),
        name="ffn",
    )(x, mods, gn, w_in, w_out)


def _gla_proj_kernel(x_ref, mod_ref, gn_ref, w_ref, wz_ref, wg_ref, bg_ref,
                     q_ref, k_ref, v_ref, r_ref, la_ref):
    h = _pre(x_ref[...], mod_ref, gn_ref, 1).astype(BF16)
    z = _dot(h, wz_ref[...]).astype(BF16)
    r_ref[...] = _silu(_dot(h, w_ref[:, 2 * GLA_QK + GLA_VV:2 * GLA_QK + 2 * GLA_VV])).astype(BF16)
    logit = _dot(z, wg_ref[...]) + bg_ref[...]
    la_ref[...] = (jnp.minimum(logit, 0.0) - jnp.log(1.0 + jnp.exp(-jnp.abs(logit)))) * (LOG2_E / GLA_TAU)
    q_ref[...] = _dot(h, w_ref[:, 0:GLA_QK]) * (GLA_DK ** -0.5)
    k_ref[...] = _dot(h, w_ref[:, GLA_QK:2 * GLA_QK])
    v_ref[...] = _dot(h, w_ref[:, 2 * GLA_QK:2 * GLA_QK + GLA_VV]).astype(BF16)


def _gla_proj(st, x, mods, gn, w_qkvr, w_z, w_gate_bd, b_gate):
    n = st.n
    return pl.pallas_call(
        _gla_proj_kernel,
        grid=(st.tiles,),
        in_specs=[st.tok(D_MODEL), st.mod(), _resident((2, D_MODEL)),
                  _resident(w_qkvr.shape), _resident(w_z.shape),
                  _resident(w_gate_bd.shape), _resident(b_gate.shape)],
        out_specs=[st.tok(GLA_QK), st.tok(GLA_QK), st.tok(GLA_VV), st.tok(GLA_VV), st.tok(2 * GLA_QK)],
        out_shape=[jax.ShapeDtypeStruct((n, GLA_QK), F32), jax.ShapeDtypeStruct((n, GLA_QK), F32),
                   jax.ShapeDtypeStruct((n, GLA_VV), BF16), jax.ShapeDtypeStruct((n, GLA_VV), BF16),
                   jax.ShapeDtypeStruct((n, 2 * GLA_QK), F32)],
        compiler_params=_params("parallel"),
        name="gla_proj",
    )(x, mods, gn, w_qkvr, w_z, w_gate_bd, b_gate)


def _gla_scan_kernel(*refs, has_state, want_state):
    it = iter(refs)
    q_ref, k_ref, v_ref, la_ref = next(it), next(it), next(it), next(it)
    s0_ref = next(it) if has_state else None
    o_ref = next(it)
    sout_ref = next(it) if want_state else None
    st_ref = next(it)

    d = pl.program_id(1)
    blk = pl.program_id(2)
    n_chunks = q_ref.shape[0] // GLA_CHUNK
    head_k = [slice(h * GLA_DK, (h + 1) * GLA_DK) for h in range(GLA_HEADS)]
    head_v = [slice(h * GLA_DV, (h + 1) * GLA_DV) for h in range(GLA_HEADS)]

    @pl.when(blk == 0)
    def _():
        if has_state:
            st_ref[...] = s0_ref[...]
        else:
            st_ref[...] = jnp.zeros(st_ref.shape, F32)

    ii = lax.broadcasted_iota(jnp.int32, (GLA_CHUNK, GLA_CHUNK), 0)
    jj = lax.broadcasted_iota(jnp.int32, (GLA_CHUNK, GLA_CHUNK), 1)

    n_groups = GLA_CHUNK // 8
    sub = lax.broadcasted_iota(jnp.int32, (n_groups, 8, GLA_QK), 1)

    def cumsum(x, backward):
        g = x.reshape(n_groups, 8, GLA_QK)
        for step in (1, 2, 4):
            if backward:
                g = g + jnp.where(sub < 8 - step, pltpu.roll(g, 8 - step, axis=1), 0.0)
            else:
                g = g + jnp.where(sub >= step, pltpu.roll(g, step, axis=1), 0.0)
        edge = 0 if backward else 7
        order = range(n_groups - 1, -1, -1) if backward else range(n_groups)
        out, carry = [None] * n_groups, None
        for i in order:
            out[i] = g[i] if carry is None else g[i] + carry
            carry = jnp.broadcast_to(out[i][edge:edge + 1, :], (8, GLA_QK))
        return jnp.concatenate(out, axis=0)

    def pairwise_scores(q, k, b, mask):
        lane = lax.broadcasted_iota(jnp.int32, (GLA_CHUNK, GLA_CHUNK), 1)
        rowi = lax.broadcasted_iota(jnp.int32, (GLA_CHUNK, GLA_DK), 0)
        out = []
        for sk in head_k:
            def key_row(j, att, sk=sk):
                pick = rowi == j
                bj = jnp.sum(jnp.where(pick, b[:, sk], 0.0), axis=0, keepdims=True)
                kj = jnp.sum(jnp.where(pick, k[:, sk], 0.0), axis=0, keepdims=True)
                col = jnp.sum(q[:, sk] * jnp.exp2(jnp.minimum(b[:, sk] - bj, 0.0)) * kj, axis=1, keepdims=True)
                return att + jnp.where(lane == j, col, 0.0)
            att = lax.fori_loop(0, GLA_CHUNK, key_row, jnp.zeros((GLA_CHUNK, GLA_CHUNK), F32))
            out.append(jnp.where(mask, att, 0.0).astype(BF16))
        return out

    def run(backward, factored):
        mask = (jj >= ii) if backward else (jj <= ii)
        order = list(range(n_chunks - 1, -1, -1) if backward else range(n_chunks))

        def prep(c):
            rows = slice(c * GLA_CHUNK, (c + 1) * GLA_CHUNK)
            la = la_ref[rows, :]
            b = cumsum(la, backward)
            btot = jnp.sum(la, axis=0, keepdims=True)
            q = q_ref[rows, :]
            k = k_ref[rows, :]
            if factored:
                bm = b[GLA_CHUNK // 2:GLA_CHUNK // 2 + 1, :]
                qm = (q * jnp.exp2(b - bm)).astype(BF16)
                km = (k * jnp.exp2(bm - b)).astype(BF16)
                att = [jnp.where(mask, _dot_nt(qm[:, sk], km[:, sk]), 0.0).astype(BF16) for sk in head_k]
            else:
                att = pairwise_scores(q, k, b, mask)
            return dict(rows=rows, att=att, v=v_ref[rows, :], qb=(q * jnp.exp2(b)).astype(BF16),
                        kt=(k * jnp.exp2(btot - b)).astype(BF16), dec=jnp.exp2(btot))

        def update(p):
            for h, (sk, sv) in enumerate(zip(head_k, head_v)):
                s = st_ref[h]
                v = p["v"][:, sv]
                o_ref[p["rows"], sv] = (_dot(p["att"][h], v) + _dot(p["qb"][:, sk], s.astype(BF16))).astype(BF16)
                dec_t = jnp.broadcast_to(p["dec"][:, sk], (GLA_DK, GLA_DK)).T
                st_ref[h] = s * jnp.concatenate([dec_t] * (GLA_DV // GLA_DK), axis=1) + _dot_tn(p["kt"][:, sk], v)

        nxt = prep(order[0])
        for n in range(n_chunks):
            cur, nxt = nxt, (prep(order[n + 1]) if n + 1 < n_chunks else None)
            update(cur)

    totals = jnp.sum(la_ref[...].reshape(n_chunks, GLA_CHUNK, GLA_QK), axis=1)
    safe = jnp.min(totals) > -GLA_SAFE_LOG2
    for backward in (False, True):
        for factored in (True, False):
            @pl.when(jnp.logical_and(d == int(backward), safe if factored else jnp.logical_not(safe)))
            def _(backward=backward, factored=factored):
                run(backward, factored)

    if want_state:
        @pl.when(blk == pl.num_programs(2) - 1)
        def _():
            sout_ref[...] = st_ref[...]


def _gla_scan(st, q, k, v, la, s0, want_state):
    block = min(GLA_BLOCK, st.seq)
    nb = st.seq // block
    has_state = s0 is not None

    def row(b, d, i):
        return b * nb + i + d * (nb - 1 - 2 * i)

    state_spec = pl.BlockSpec((None, None, GLA_HEADS, GLA_DK, GLA_DV), lambda b, d, i: (b, d, 0, 0, 0))
    in_specs = [pl.BlockSpec((block, GLA_QK), lambda b, d, i: (row(b, d, i), 0)),
                pl.BlockSpec((block, GLA_QK), lambda b, d, i: (row(b, d, i), 0)),
                pl.BlockSpec((block, GLA_VV), lambda b, d, i: (row(b, d, i), 0)),
                pl.BlockSpec((block, GLA_QK), lambda b, d, i: (row(b, d, i), d))]
    args = [q, k, v, la]
    if has_state:
        in_specs.append(state_spec)
        args.append(s0)
    out_specs = [pl.BlockSpec((None, block, GLA_VV), lambda b, d, i: (d, row(b, d, i), 0))]
    out_shape = [jax.ShapeDtypeStruct((2, st.n, GLA_VV), BF16)]
    if want_state:
        out_specs.append(state_spec)
        out_shape.append(jax.ShapeDtypeStruct((st.batch, 2, GLA_HEADS, GLA_DK, GLA_DV), F32))
    return pl.pallas_call(
        functools.partial(_gla_scan_kernel, has_state=has_state, want_state=want_state),
        grid=(st.batch, 2, nb),
        in_specs=in_specs,
        out_specs=out_specs,
        out_shape=out_shape,
        scratch_shapes=[pltpu.VMEM((GLA_HEADS, GLA_DK, GLA_DV), F32)],
        compiler_params=_params("parallel", "parallel", "arbitrary"),
        name="gla_scan",
    )(*args)


def _gla_post_ffn_kernel(o_ref, r_ref, x_ref, mod_ref, gn_ref, go_ref, w_ref, gn2_ref, win_ref, wout_ref, out_ref):
    o = o_ref[0].astype(F32) + o_ref[1].astype(F32)
    heads = []
    for h in range(GLA_HEADS):
        sv = slice(h * GLA_DV, (h + 1) * GLA_DV)
        heads.append(_rms(o[:, sv], go_ref[:, sv]))
    g = (jnp.concatenate(heads, axis=1) * r_ref[...].astype(F32)).astype(BF16)
    x = _residual(x_ref[...], _dot(g, w_ref[...]), mod_ref, gn_ref, 1, 1.0)
    out_ref[...] = _swiglu_sublayer(x, mod_ref, gn2_ref, win_ref, wout_ref, 2)


def _gla_post_ffn(st, o, r, x, mods, gn, g_out, w_out, gn2, w_ffn_in, w_ffn_out, layer):
    return pl.pallas_call(
        _gla_post_ffn_kernel,
        grid=(st.tiles,),
        in_specs=[pl.BlockSpec((2, TOKEN_TILE, GLA_VV), lambda i: (0, i, 0)),
                  st.tok(GLA_VV), st.tok(D_MODEL), st.mod(), _resident((2, D_MODEL)),
                  _resident((1, GLA_VV)), _resident((GLA_VV, D_MODEL)), _resident((2, D_MODEL))]
                 + _ffn_specs(layer, 1),
        out_specs=st.tok(D_MODEL),
        out_shape=jax.ShapeDtypeStruct((st.n, D_MODEL), F32),
        compiler_params=_params("parallel"),
        name="gla_post_ffn",
    )(o, r, x, mods, gn, g_out, w_out, gn2, w_ffn_in, w_ffn_out)


MLA_Q_SCALE = MLA_SCALE * LOG2_E


def _mla_proj_kernel(x_ref, mod_ref, gn_ref, w_ref, gq_ref, gkv_ref, wuqt_ref, tabqt_ref,
                     qt_ref, ckv_ref, kr_ref):
    h = _pre(x_ref[...], mod_ref, gn_ref, 1).astype(BF16)
    p = _dot(h, w_ref[...])
    ckv_ref[...] = _rms(p[:, MLA_Q_RANK:MLA_Q_RANK + MLA_KV_RANK], gkv_ref[...])
    kr_ref[...] = p[:, MLA_Q_RANK + MLA_KV_RANK:]
    cqt = _rms(p[:, 0:MLA_Q_RANK], gq_ref[...]).T.astype(BF16)
    tab = tabqt_ref[...] * MLA_Q_SCALE
    for hd in range(MLA_HEADS):
        qt = _dot(wuqt_ref[hd], cqt)
        for j in range(TOKEN_TILE // MLA_Q_SUB):
            cols = slice(j * MLA_Q_SUB, (j + 1) * MLA_Q_SUB)
            qt_ref[j, hd] = jnp.concatenate([qt[0:MLA_NOPE, cols] * MLA_Q_SCALE, qt[MLA_NOPE:, cols] * tab[:, cols]],
                                            axis=0).astype(BF16)


def _mla_proj(st, x, mods, gn, w_in_ext, g_q, g_kv, w_uqt, tabqt):
    n = st.n
    n_sub = TOKEN_TILE // MLA_Q_SUB
    tab_tiles = tabqt.shape[1] // TOKEN_TILE
    return pl.pallas_call(
        _mla_proj_kernel,
        grid=(st.tiles,),
        in_specs=[st.tok(D_MODEL), st.mod(), _resident((2, D_MODEL)), _resident(w_in_ext.shape),
                  _resident((1, MLA_Q_RANK)), _resident((1, MLA_KV_RANK)), _resident(w_uqt.shape),
                  pl.BlockSpec((2 * MLA_ROPE, TOKEN_TILE), lambda i: (0, i % tab_tiles))],
        out_specs=[pl.BlockSpec((n_sub, MLA_HEADS, 256, MLA_Q_SUB), lambda i: (i, 0, 0, 0)),
                   st.tok(MLA_KV_RANK), st.tok(2 * MLA_ROPE)],
        out_shape=[jax.ShapeDtypeStruct((n // MLA_Q_SUB, MLA_HEADS, 256, MLA_Q_SUB), BF16),
                   jax.ShapeDtypeStruct((n, MLA_KV_RANK), F32),
                   jax.ShapeDtypeStruct((n, 2 * MLA_ROPE), F32)],
        compiler_params=_params("parallel"),
        name="mla_proj",
    )(x, mods, gn, w_in_ext, g_q, g_kv, w_uqt, tabqt)


def _mla_build_kv(ckv, kr_dup, wukv, k_dst, vt_dst):
    k_dst[:, MLA_NOPE:] = kr_dup
    for lo, hi in _mla_key_parts(ckv.shape[0]):
        kv = _dot(ckv[lo:hi, :], wukv)
        k_dst[lo:hi, 0:MLA_NOPE] = kv[:, 0:MLA_NOPE].astype(BF16)
        vt_dst[:, lo:hi] = kv[:, MLA_NOPE:].T.astype(BF16)


def _mla_key_parts(tk):
    half = (tk // 2 + 255) // 256 * 256
    return [(0, tk)] if tk <= 256 else [(0, half), (half, tk)]


def _mla_colmax(sts):
    return functools.reduce(jnp.maximum, [jnp.max(s, axis=0, keepdims=True) for s in sts])


def _mla_softmax_pv(sts, m, vt_parts):
    pts = [jnp.exp2(s - m) for s in sts]
    denom = functools.reduce(jnp.add, [jnp.sum(p, axis=0, keepdims=True) for p in pts])
    ot = functools.reduce(jnp.add, [_dot(vt, p.astype(BF16)) for vt, p in zip(vt_parts, pts)])
    return (ot / denom).T.astype(BF16)


def _mla_attn_kernel(qt_ref, ckv_ref, kr_ref, wukv_ref, o_ref, k_scr, vt_scr):
    kr = kr_ref[...]
    kr_dup = (kr + pltpu.roll(kr, MLA_ROPE, axis=1)).astype(BF16)
    ckv = ckv_ref[...]

    def scores(h):
        _mla_build_kv(ckv, kr_dup, wukv_ref[h], k_scr.at[h], vt_scr.at[h])
        return _dot(k_scr[h], qt_ref[h])

    nxt = scores(0)
    for h in range(MLA_HEADS):
        cur, nxt = nxt, (scores(h + 1) if h + 1 < MLA_HEADS else None)
        o_ref[:, h * MLA_V:(h + 1) * MLA_V] = _mla_softmax_pv([cur], _mla_colmax([cur]), [vt_scr[h]])


def _mla_attn(batch, seq, qt, ckv3, kr_masked, w_ukv):
    return pl.pallas_call(
        _mla_attn_kernel,
        grid=(batch,),
        in_specs=[pl.BlockSpec((None, MLA_HEADS, 256, seq), lambda b: (b, 0, 0, 0)),
                  pl.BlockSpec((None, seq, MLA_KV_RANK), lambda b: (b, 0, 0)),
                  pl.BlockSpec((None, seq, 2 * MLA_ROPE), lambda b: (b, 0, 0)),
                  _resident(w_ukv.shape)],
        out_specs=pl.BlockSpec((seq, MLA_HEADS * MLA_V), lambda b: (b, 0)),
        out_shape=jax.ShapeDtypeStruct((batch * seq, MLA_HEADS * MLA_V), BF16),
        scratch_shapes=[pltpu.VMEM((MLA_HEADS, seq, 256), BF16), pltpu.VMEM((MLA_HEADS, MLA_V, seq), BF16)],
        compiler_params=_params("parallel"),
        name="mla_attn",
    )(qt, ckv3, kr_masked, w_ukv)


def _mla_attn_long_kernel(qt_ref, ckv_ref, kr_ref, tabk_ref, wukv_ref, o_ref,
                          k_scr, vt_scr, st_scr, m_scr, acc_scr, den_scr):
    n_sub = qt_ref.shape[0]
    tk = k_scr.shape[0]
    tiles = [(lo, min(lo + MLA_KEY_TILE, tk)) for lo in range(0, tk, MLA_KEY_TILE)]
    halves = [[(lo, (lo + hi) // 2), ((lo + hi) // 2, hi)] for lo, hi in tiles]

    y = kr_ref[...] * tabk_ref[...]
    _mla_build_kv(ckv_ref[...], (y + pltpu.roll(y, MLA_ROPE, axis=1)).astype(BF16), wukv_ref[...], k_scr, vt_scr)

    def fold8(x, op):
        return op(x.reshape(x.shape[0] // 8, 8, x.shape[1]), axis=0)

    def finalize(j, slot):
        start = j * MLA_Q_SUB
        rows = pl.ds(start if isinstance(j, int) else pl.multiple_of(start, MLA_Q_SUB), MLA_Q_SUB)
        o_ref[rows, :] = (acc_scr[slot] / jnp.sum(den_scr[slot], axis=0, keepdims=True)).T.astype(BF16)

    def stage(j_next, slot_next, j_cur, slot_cur, j_done):
        qt = None if j_next is None else qt_ref[j_next]
        if j_cur is not None:
            m = jnp.max(m_scr[slot_cur], axis=0, keepdims=True)
            acc = jnp.zeros((MLA_V, MLA_Q_SUB), F32)
            den = jnp.zeros((8, MLA_Q_SUB), F32)
        run_max = None
        for n, pieces in enumerate(halves):
            for lo, hi in pieces:
                if qt is not None:
                    s = _dot(k_scr[lo:hi, :], qt)
                    st_scr[slot_next, lo:hi, :] = s
                    s8 = fold8(s, jnp.max)
                    run_max = s8 if run_max is None else jnp.maximum(run_max, s8)
            for lo, hi in pieces:
                if j_cur is not None:
                    p = jnp.exp2(st_scr[slot_cur, lo:hi, :] - m)
                    den = den + fold8(p, jnp.sum)
                    acc = acc + _dot(vt_scr[:, lo:hi], p.astype(BF16))
            if n == 0 and j_done is not None:
                finalize(j_done, 1 - slot_cur)
        if qt is not None:
            m_scr[slot_next] = run_max
        if j_cur is not None:
            acc_scr[slot_cur] = acc
            den_scr[slot_cur] = den

    acc_scr[1] = jnp.zeros(acc_scr.shape[1:], F32)
    den_scr[1] = jnp.ones(den_scr.shape[1:], F32)
    stage(0, 0, None, None, None)

    def trip(j):
        for i in range(MLA_UNROLL):
            last = isinstance(j, int) and j + i + 1 == n_sub
            done = max(j + i - 1, 0) if isinstance(j, int) else jnp.maximum(j + i - 1, 0)
            stage(None if last else j + i + 1, (i + 1) % 2, j + i, i % 2, done)

    def body(t, carry):
        trip(MLA_UNROLL * t)
        return carry

    lax.fori_loop(0, n_sub // MLA_UNROLL - 1, body, 0)
    trip(n_sub - MLA_UNROLL)
    finalize(n_sub - 1, (n_sub - 1) % 2)


def _mla_attn_long(batch, tq_total, qt, ckv_all, kr_all, tabk, w_ukv):
    tk = ckv_all.shape[1]
    n_sub = tq_total // MLA_Q_SUB
    once = pl.Buffered(1)
    return pl.pallas_call(
        _mla_attn_long_kernel,
        grid=(batch, MLA_HEADS),
        in_specs=[pl.BlockSpec((n_sub, None, 256, MLA_Q_SUB), lambda b, h: (b, h, 0, 0)),
                  pl.BlockSpec((None, tk, MLA_KV_RANK), lambda b, h: (b, 0, 0), pipeline_mode=once),
                  pl.BlockSpec((None, tk, 2 * MLA_ROPE), lambda b, h: (b, 0, 0), pipeline_mode=once),
                  pl.BlockSpec((tk, 2 * MLA_ROPE), lambda b, h: (0, 0), pipeline_mode=once),
                  pl.BlockSpec((None, MLA_KV_RANK, 256), lambda b, h: (h, 0, 0))],
        out_specs=pl.BlockSpec((tq_total, MLA_V), lambda b, h: (b, h)),
        out_shape=jax.ShapeDtypeStruct((batch * tq_total, MLA_HEADS * MLA_V), BF16),
        scratch_shapes=[pltpu.VMEM((tk, 256), BF16), pltpu.VMEM((MLA_V, tk), BF16),
                        pltpu.VMEM((2, tk, MLA_Q_SUB), F32), pltpu.VMEM((2, 8, MLA_Q_SUB), F32),
                        pltpu.VMEM((2, MLA_V, MLA_Q_SUB), F32), pltpu.VMEM((2, 8, MLA_Q_SUB), F32)],
        compiler_params=_params("parallel", "arbitrary"),
        name="mla_attn_long",
    )(qt, ckv_all, kr_all, tabk, w_ukv)


def _mla_post_ffn_kernel(a_ref, x_ref, mod_ref, gn_ref, w_ref, gn2_ref, win_ref, wout_ref, out_ref):
    x = _residual(x_ref[...], _dot(a_ref[...], w_ref[...]), mod_ref, gn_ref, 1, 1.0)
    out_ref[...] = _swiglu_sublayer(x, mod_ref, gn2_ref, win_ref, wout_ref, 2)


def _mla_post_ffn(st, a, x, mods, gn, w_out, gn2, w_ffn_in, w_ffn_out, layer):
    return pl.pallas_call(
        _mla_post_ffn_kernel,
        grid=(st.tiles,),
        in_specs=[st.tok(MLA_HEADS * MLA_V), st.tok(D_MODEL), st.mod(), _resident((2, D_MODEL)),
                  _resident((MLA_HEADS * MLA_V, D_MODEL)), _resident((2, D_MODEL))] + _ffn_specs(layer, 1),
        out_specs=st.tok(D_MODEL),
        out_shape=jax.ShapeDtypeStruct((st.n, D_MODEL), F32),
        compiler_params=_params("parallel"),
        name="mla_post_ffn",
    )(a, x, mods, gn, w_out, gn2, w_ffn_in, w_ffn_out)


_ROPE_SWAP = tuple(list(range(16, 32)) + list(range(0, 16)) + list(range(48, 64)) + list(range(32, 48)))


def _rope_table(n_tokens):
    rows = n_tokens // GRID_W
    row = np.repeat(np.arange(rows), GRID_W).astype(np.float64)
    col = np.tile(np.arange(GRID_W), rows).astype(np.float64)
    n_pairs = MLA_ROPE // 4
    inv = ROPE_BASE ** (-np.arange(n_pairs, dtype=np.float64) / n_pairs)
    ar, ac = row[:, None] * inv, col[:, None] * inv
    cos = np.concatenate([np.cos(ar), np.cos(ar), np.cos(ac), np.cos(ac)], axis=1)
    sin = np.concatenate([-np.sin(ar), np.sin(ar), -np.sin(ac), np.sin(ac)], axis=1)
    return np.concatenate([cos, sin], axis=1).astype(np.float32)


def _identity_table(n_tokens):
    return np.concatenate([np.ones((n_tokens, MLA_ROPE), np.float32), np.zeros((n_tokens, MLA_ROPE), np.float32)], axis=1)


def kernel(x_prompt, x_sample, state_gla, cache_mla_ckv, cache_mla_krope, c, c_ctx, w_mod, b_mod, g_norm, w_ffn_in, w_ffn_out, gla_w_in, gla_w_gate, gla_b_gate, gla_g_out, gla_w_out, mla_w_in, mla_g_q, mla_g_kv, mla_w_uq, mla_w_ukv, mla_w_out):
    batch, seq = x_prompt.shape[0], x_prompt.shape[1]
    dec_batch, dec_seq = x_sample.shape[0], x_sample.shape[1]
    past = cache_mla_ckv.shape[2]
    ctx = _Stream(batch, seq, dec_batch, per_batch_mod=False)
    lat = _Stream(dec_batch, dec_seq, 0, per_batch_mod=True)

    cond8 = jnp.concatenate([c, c_ctx[None, :], jnp.zeros((8 - dec_batch - 1, D_MODEL), F32)], axis=0)
    mods = _modulation(cond8, w_mod, b_mod)

    xs = {ctx: x_prompt.reshape(ctx.n, D_MODEL), lat: x_sample.reshape(lat.n, D_MODEL)}
    w_ffn_in_b = w_ffn_in.astype(BF16)
    w_ffn_out_b = w_ffn_out.astype(BF16)
    gla_states, mla_ckvs, mla_krs = [], [], []

    for l in range(DEPTH):
        j = l // 2
        for st in (ctx, lat):
            xs[st] = _ffn(st, xs[st], mods[l], g_norm[l, 0], w_ffn_in_b, w_ffn_out_b, l, 0, 0)

        if l % 2 == 0:
            w_in = gla_w_in[j].astype(BF16)
            w_qkvr, w_z = w_in[:, :2 * GLA_QK + 2 * GLA_VV], w_in[:, 2 * GLA_QK + 2 * GLA_VV:]
            zeros = jnp.zeros((GLA_GATE_RANK, GLA_QK), F32)
            w_gate_bd = jnp.concatenate([jnp.concatenate([gla_w_gate[j, 0], zeros], axis=1),
                                         jnp.concatenate([zeros, gla_w_gate[j, 1]], axis=1)], axis=0).astype(BF16)
            b_gate = gla_b_gate[j].reshape(1, 2 * GLA_QK)
            g_out = gla_g_out[j].reshape(1, GLA_VV)
            w_out = gla_w_out[j].astype(BF16)
            for st in (ctx, lat):
                q, k, v, r, la = _gla_proj(st, xs[st], mods[l], g_norm[l, 1], w_qkvr, w_z, w_gate_bd, b_gate)
                if st is ctx:
                    o, s_fin = _gla_scan(st, q, k, v, la, None, True)
                    gla_states.append(s_fin)
                else:
                    (o,) = _gla_scan(st, q, k, v, la, state_gla[:, j].astype(F32), False)
                xs[st] = _gla_post_ffn(st, o, r, xs[st], mods[l], g_norm[l, 1], g_out, w_out,
                                       g_norm[l, 2], w_ffn_in_b, w_ffn_out_b, l)
        else:
            swap = jnp.array(_ROPE_SWAP)
            w_in = mla_w_in[j]
            w_in_ext = jnp.concatenate([w_in, w_in[:, MLA_Q_RANK + MLA_KV_RANK:][:, swap]], axis=1).astype(BF16)
            w_uq = mla_w_uq[j].reshape(MLA_Q_RANK, MLA_HEADS, MLA_NOPE + MLA_ROPE)
            w_uq = jnp.concatenate([w_uq, w_uq[:, :, MLA_NOPE:][:, :, swap]], axis=2)
            w_uqt = jnp.transpose(w_uq, (1, 2, 0)).astype(BF16)
            w_ukv = jnp.transpose(mla_w_ukv[j].reshape(MLA_KV_RANK, MLA_HEADS, MLA_NOPE + MLA_V), (1, 0, 2)).astype(BF16)
            g_q = mla_g_q[j].reshape(1, MLA_Q_RANK)
            g_kv = mla_g_kv[j].reshape(1, MLA_KV_RANK)
            w_out = mla_w_out[j].astype(BF16)
            tabq = _rope_table(lat.seq)
            tabqt = {ctx: jnp.asarray(_identity_table(TOKEN_TILE).T), lat: jnp.asarray(tabq.T)}
            for st in (ctx, lat):
                qt, ckv, kr2 = _mla_proj(st, xs[st], mods[l], g_norm[l, 1], w_in_ext, g_q, g_kv, w_uqt, tabqt[st])
                ckv3 = ckv.reshape(st.batch, st.seq, MLA_KV_RANK)
                kr3 = kr2.reshape(st.batch, st.seq, 2 * MLA_ROPE)
                if st is ctx:
                    mla_ckvs.append(ckv3)
                    kr = kr3[:, :, :MLA_ROPE]
                    mla_krs.append(kr)
                    kr_masked = jnp.concatenate([kr, jnp.zeros_like(kr)], axis=2)
                    a = _mla_attn(st.batch, st.seq, qt, ckv3.astype(BF16), kr_masked, w_ukv)
                else:
                    ckv_all = jnp.concatenate([cache_mla_ckv[:, j].astype(F32), ckv3], axis=1).astype(BF16)
                    kr_ctx = cache_mla_krope[:, j].astype(F32)
                    kr_all = jnp.concatenate([jnp.concatenate([kr_ctx, jnp.zeros_like(kr_ctx)], axis=2), kr3], axis=1)
                    tabk = jnp.asarray(np.concatenate([_identity_table(past), tabq], axis=0))
                    a = _mla_attn_long(st.batch, st.seq, qt, ckv_all, kr_all, tabk, w_ukv)
                xs[st] = _mla_post_ffn(st, a, xs[st], mods[l], g_norm[l, 1], w_out,
                                       g_norm[l, 2], w_ffn_in_b, w_ffn_out_b, l)

    yp = xs[ctx].reshape(batch, seq, D_MODEL)
    ys = xs[lat].reshape(dec_batch, dec_seq, D_MODEL)
    new_state_gla = jnp.stack(gla_states, axis=1).astype(x_prompt.dtype)
    new_cache_mla_ckv = jnp.stack(mla_ckvs, axis=1)
    new_cache_mla_krope = jnp.stack(mla_krs, axis=1)
    return (yp, ys, new_state_gla, new_cache_mla_ckv, new_cache_mla_krope)
```

```python
import functools

import jax
import jax.numpy as jnp
import numpy as np
from jax import lax
from jax.experimental import pallas as pl
from jax.experimental.pallas import tpu as pltpu

F32 = jnp.float32
BF16 = jnp.bfloat16

D_MODEL = 1024
DEPTH = 2
GRID_W = 64
D_FF = 2816
MACARON_W = 0.5
N_MOD = 9
EPS = 1e-6
LOG2_E = 1.4426950408889634

GLA_HEADS = 4
GLA_DK = 128
GLA_DV = 256
GLA_GATE_RANK = 16
GLA_TAU = 16.0
GLA_QK = GLA_HEADS * GLA_DK
GLA_VV = GLA_HEADS * GLA_DV

MLA_HEADS = 16
MLA_NOPE = 128
MLA_ROPE = 64
MLA_V = 128
MLA_Q_RANK = 512
MLA_KV_RANK = 256
MLA_SCALE = (MLA_NOPE + MLA_ROPE) ** -0.5
MLA_QK_DIM = MLA_NOPE + 2 * MLA_ROPE
MLA_KV_DIM = MLA_NOPE + MLA_V
ROPE_BASE = 10000.0

VMEM_LIMIT_BYTES = 56 * 1024 * 1024
MXU_DIM = 256
SUBLANES = 8

TOKEN_TILE = 512
GLA_CHUNK = 64
GLA_BLOCK = 512
GLA_SAFE_LOG2 = 120.0
MLA_Q_SUB = MXU_DIM
MLA_KEY_TILE = 512
MLA_UNROLL = 4


def _params(*sem):
    return pltpu.CompilerParams(dimension_semantics=sem, vmem_limit_bytes=VMEM_LIMIT_BYTES)


def _resident(shape):
    nd = len(shape)
    return pl.BlockSpec(shape, lambda *_: (0,) * nd, pipeline_mode=pl.Buffered(1))


def _silu(x):
    return x * (1.0 / (1.0 + jnp.exp(-x)))


def _rms(x, g):
    return x * lax.rsqrt(jnp.mean(x * x, axis=-1, keepdims=True) + EPS) * g


def _pre(x, mod_ref, gn_ref, s):
    return _rms(x, gn_ref[0:1, :]) * (1.0 + mod_ref[3 * s + 1]) + mod_ref[3 * s]


def _residual(x, y, mod_ref, gn_ref, s, weight):
    return x + (weight * mod_ref[3 * s + 2]) * _rms(y, gn_ref[1:2, :])


def _dot(a, b):
    return jnp.dot(a, b, preferred_element_type=F32)


def _dot_nt(a, b):
    return lax.dot_general(a, b, (((1,), (1,)), ((), ())), preferred_element_type=F32)


def _dot_tn(a, b):
    return lax.dot_general(a, b, (((0,), (0,)), ((), ())), preferred_element_type=F32)


def _mod_kernel(cond_ref, w_ref, b_ref, o_ref):
    a = _silu(cond_ref[...]).astype(BF16)
    o_ref[...] = _dot(a, w_ref[...].astype(BF16)) + b_ref[...]


def _modulation(cond8, w_mod, b_mod):
    out = pl.pallas_call(
        _mod_kernel,
        grid=(DEPTH, N_MOD),
        in_specs=[
            pl.BlockSpec((SUBLANES, D_MODEL), lambda l, j: (0, 0)),
            pl.BlockSpec((None, D_MODEL, D_MODEL), lambda l, j: (l, 0, j)),
            pl.BlockSpec((None, None, 1, D_MODEL), lambda l, j: (l, j, 0, 0)),
        ],
        out_specs=pl.BlockSpec((None, None, SUBLANES, D_MODEL), lambda l, j: (l, j, 0, 0)),
        out_shape=jax.ShapeDtypeStruct((DEPTH, N_MOD, SUBLANES, D_MODEL), F32),
        compiler_params=_params("parallel", "parallel"),
        name="modulation",
    )(cond8, w_mod, b_mod.reshape(DEPTH, N_MOD, 1, D_MODEL))
    return out.reshape(DEPTH, N_MOD, SUBLANES, 1, D_MODEL)


class _Stream:
    def __init__(self, batch, seq, mod_row0, per_batch_mod):
        self.batch, self.seq = batch, seq
        self.n = batch * seq
        self.tiles = self.n // TOKEN_TILE
        tiles_per_batch = seq // TOKEN_TILE
        if per_batch_mod:
            self.mod_row = lambda i: mod_row0 + i // tiles_per_batch
        else:
            self.mod_row = lambda i: mod_row0

    def tok(self, width):
        return pl.BlockSpec((TOKEN_TILE, width), lambda i: (i, 0))

    def mod(self):
        return pl.BlockSpec((N_MOD, None, 1, D_MODEL), lambda i: (0, self.mod_row(i), 0, 0))


def _swiglu_sublayer(x, mod_ref, gn_ref, win_ref, wout_ref, s):
    half = TOKEN_TILE // 2
    xs = [x[0:half, :], x[half:, :]]
    ups = []
    for xh in xs:
        h = _pre(xh, mod_ref, gn_ref, s).astype(BF16)
        ups.append((_dot(h, win_ref[:, 0:D_FF]), _dot(h, win_ref[:, D_FF:])))
    outs = []
    for xh, (g, u) in zip(xs, ups):
        y = _dot((_silu(g) * u).astype(BF16), wout_ref[...])
        outs.append(_residual(xh, y, mod_ref, gn_ref, s, MACARON_W))
    return jnp.concatenate(outs, axis=0)


def _ffn_specs(layer, which):
    once = pl.Buffered(1)
    return [pl.BlockSpec((None, None, D_MODEL, 2 * D_FF), lambda i: (layer, which, 0, 0), pipeline_mode=once),
            pl.BlockSpec((None, None, D_FF, D_MODEL), lambda i: (layer, which, 0, 0), pipeline_mode=once)]


def _ffn_kernel(x_ref, mod_ref, gn_ref, win_ref, wout_ref, o_ref, *, s):
    o_ref[...] = _swiglu_sublayer(x_ref[...], mod_ref, gn_ref, win_ref, wout_ref, s)


def _ffn(st, x, mods, gn, w_in, w_out, layer, which, s):
    return pl.pallas_call(
        functools.partial(_ffn_kernel, s=s),
        grid=(st.tiles,),
        in_specs=[st.tok(D_MODEL), st.mod(), _resident((2, D_MODEL))] + _ffn_specs(layer, which),
        out_specs=st.tok(D_MODEL),
        out_shape=jax.ShapeDtypeStruct((st.n, D_MODEL), F32),
        compiler_params=_params("parallel"),
        name="ffn",
    )(x, mods, gn, w_in, w_out)


def _gla_proj_kernel(x_ref, mod_ref, gn_ref, w_ref, wz_ref, wg_ref, bg_ref,
                     q_ref, k_ref, v_ref, r_ref, la_ref):
    h = _pre(x_ref[...], mod_ref, gn_ref, 1).astype(BF16)
    z = _dot(h, wz_ref[...]).astype(BF16)
    r_ref[...] = _silu(_dot(h, w_ref[:, 2 * GLA_QK + GLA_VV:2 * GLA_QK + 2 * GLA_VV])).astype(BF16)
    logit = _dot(z, wg_ref[...]) + bg_ref[...]
    la_ref[...] = (jnp.minimum(logit, 0.0) - jnp.log(1.0 + jnp.exp(-jnp.abs(logit)))) * (LOG2_E / GLA_TAU)
    q_ref[...] = _dot(h, w_ref[:, 0:GLA_QK]) * (GLA_DK ** -0.5)
    k_ref[...] = _dot(h, w_ref[:, GLA_QK:2 * GLA_QK])
    v_ref[...] = _dot(h, w_ref[:, 2 * GLA_QK:2 * GLA_QK + GLA_VV]).astype(BF16)


def _gla_proj(st, x, mods, gn, w_qkvr, w_z, w_gate_bd, b_gate):
    n = st.n
    return pl.pallas_call(
        _gla_proj_kernel,
        grid=(st.tiles,),
        in_specs=[st.tok(D_MODEL), st.mod(), _resident((2, D_MODEL)),
                  _resident(w_qkvr.shape), _resident(w_z.shape),
                  _resident(w_gate_bd.shape), _resident(b_gate.shape)],
        out_specs=[st.tok(GLA_QK), st.tok(GLA_QK), st.tok(GLA_VV), st.tok(GLA_VV), st.tok(2 * GLA_QK)],
        out_shape=[jax.ShapeDtypeStruct((n, GLA_QK), F32), jax.ShapeDtypeStruct((n, GLA_QK), F32),
                   jax.ShapeDtypeStruct((n, GLA_VV), BF16), jax.ShapeDtypeStruct((n, GLA_VV), BF16),
                   jax.ShapeDtypeStruct((n, 2 * GLA_QK), F32)],
        compiler_params=_params("parallel"),
        name="gla_proj",
    )(x, mods, gn, w_qkvr, w_z, w_gate_bd, b_gate)


def _gla_scan_kernel(*refs, has_state, want_state):
    it = iter(refs)
    q_ref, k_ref, v_ref, la_ref = next(it), next(it), next(it), next(it)
    s0_ref = next(it) if has_state else None
    o_ref = next(it)
    sout_ref = next(it) if want_state else None
    st_ref = next(it)

    d = pl.program_id(1)
    blk = pl.program_id(2)
    n_chunks = q_ref.shape[0] // GLA_CHUNK
    head_k = [slice(h * GLA_DK, (h + 1) * GLA_DK) for h in range(GLA_HEADS)]
    head_v = [slice(h * GLA_DV, (h + 1) * GLA_DV) for h in range(GLA_HEADS)]

    @pl.when(blk == 0)
    def _():
        if has_state:
            st_ref[...] = s0_ref[...]
        else:
            st_ref[...] = jnp.zeros(st_ref.shape, F32)

    ii = lax.broadcasted_iota(jnp.int32, (GLA_CHUNK, GLA_CHUNK), 0)
    jj = lax.broadcasted_iota(jnp.int32, (GLA_CHUNK, GLA_CHUNK), 1)

    n_groups = GLA_CHUNK // SUBLANES
    sub = lax.broadcasted_iota(jnp.int32, (n_groups, SUBLANES, GLA_QK), 1)

    def cumsum(x, backward):
        g = x.reshape(n_groups, SUBLANES, GLA_QK)
        for step in (1, 2, 4):
            if backward:
                g = g + jnp.where(sub < SUBLANES - step, pltpu.roll(g, SUBLANES - step, axis=1), 0.0)
            else:
                g = g + jnp.where(sub >= step, pltpu.roll(g, step, axis=1), 0.0)
        edge = 0 if backward else SUBLANES - 1
        order = range(n_groups - 1, -1, -1) if backward else range(n_groups)
        out, carry = [None] * n_groups, None
        for i in order:
            out[i] = g[i] if carry is None else g[i] + carry
            carry = jnp.broadcast_to(out[i][edge:edge + 1, :], (SUBLANES, GLA_QK))
        return jnp.concatenate(out, axis=0)

    def pairwise_scores(q, k, b, mask):
        lane = lax.broadcasted_iota(jnp.int32, (GLA_CHUNK, GLA_CHUNK), 1)
        rowi = lax.broadcasted_iota(jnp.int32, (GLA_CHUNK, GLA_DK), 0)
        out = []
        for sk in head_k:
            def key_row(j, att, sk=sk):
                pick = rowi == j
                bj = jnp.sum(jnp.where(pick, b[:, sk], 0.0), axis=0, keepdims=True)
                kj = jnp.sum(jnp.where(pick, k[:, sk], 0.0), axis=0, keepdims=True)
                col = jnp.sum(q[:, sk] * jnp.exp2(jnp.minimum(b[:, sk] - bj, 0.0)) * kj, axis=1, keepdims=True)
                return att + jnp.where(lane == j, col, 0.0)
            att = lax.fori_loop(0, GLA_CHUNK, key_row, jnp.zeros((GLA_CHUNK, GLA_CHUNK), F32))
            out.append(jnp.where(mask, att, 0.0).astype(BF16))
        return out

    def run(backward, factored):
        mask = (jj >= ii) if backward else (jj <= ii)
        order = list(range(n_chunks - 1, -1, -1) if backward else range(n_chunks))

        def prep(c):
            rows = slice(c * GLA_CHUNK, (c + 1) * GLA_CHUNK)
            la = la_ref[rows, :]
            b = cumsum(la, backward)
            btot = jnp.sum(la, axis=0, keepdims=True)
            q = q_ref[rows, :]
            k = k_ref[rows, :]
            if factored:
                bm = b[GLA_CHUNK // 2:GLA_CHUNK // 2 + 1, :]
                qm = (q * jnp.exp2(b - bm)).astype(BF16)
                km = (k * jnp.exp2(bm - b)).astype(BF16)
                att = [jnp.where(mask, _dot_nt(qm[:, sk], km[:, sk]), 0.0).astype(BF16) for sk in head_k]
            else:
                att = pairwise_scores(q, k, b, mask)
            return dict(rows=rows, att=att, v=v_ref[rows, :], qb=(q * jnp.exp2(b)).astype(BF16),
                        kt=(k * jnp.exp2(btot - b)).astype(BF16), dec=jnp.exp2(btot))

        def update(p):
            for h, (sk, sv) in enumerate(zip(head_k, head_v)):
                s = st_ref[h]
                v = p["v"][:, sv]
                o_ref[p["rows"], sv] = (_dot(p["att"][h], v) + _dot(p["qb"][:, sk], s.astype(BF16))).astype(BF16)
                dec_t = jnp.broadcast_to(p["dec"][:, sk], (GLA_DK, GLA_DK)).T
                st_ref[h] = s * jnp.concatenate([dec_t] * (GLA_DV // GLA_DK), axis=1) + _dot_tn(p["kt"][:, sk], v)

        nxt = prep(order[0])
        for n in range(n_chunks):
            cur, nxt = nxt, (prep(order[n + 1]) if n + 1 < n_chunks else None)
            update(cur)

    totals = jnp.sum(la_ref[...].reshape(n_chunks, GLA_CHUNK, GLA_QK), axis=1)
    safe = jnp.min(totals) > -GLA_SAFE_LOG2
    for backward in (False, True):
        for factored in (True, False):
            @pl.when(jnp.logical_and(d == int(backward), safe if factored else jnp.logical_not(safe)))
            def _(backward=backward, factored=factored):
                run(backward, factored)

    if want_state:
        @pl.when(blk == pl.num_programs(2) - 1)
        def _():
            sout_ref[...] = st_ref[...]


def _gla_scan(st, q, k, v, la, s0, want_state):
    block = min(GLA_BLOCK, st.seq)
    nb = st.seq // block
    has_state = s0 is not None

    def row(b, d, i):
        return b * nb + i + d * (nb - 1 - 2 * i)

    state_spec = pl.BlockSpec((None, None, GLA_HEADS, GLA_DK, GLA_DV), lambda b, d, i: (b, d, 0, 0, 0))
    in_specs = [pl.BlockSpec((block, GLA_QK), lambda b, d, i: (row(b, d, i), 0)),
                pl.BlockSpec((block, GLA_QK), lambda b, d, i: (row(b, d, i), 0)),
                pl.BlockSpec((block, GLA_VV), lambda b, d, i: (row(b, d, i), 0)),
                pl.BlockSpec((block, GLA_QK), lambda b, d, i: (row(b, d, i), d))]
    args = [q, k, v, la]
    if has_state:
        in_specs.append(state_spec)
        args.append(s0)
    out_specs = [pl.BlockSpec((None, block, GLA_VV), lambda b, d, i: (d, row(b, d, i), 0))]
    out_shape = [jax.ShapeDtypeStruct((2, st.n, GLA_VV), BF16)]
    if want_state:
        out_specs.append(state_spec)
        out_shape.append(jax.ShapeDtypeStruct((st.batch, 2, GLA_HEADS, GLA_DK, GLA_DV), F32))
    return pl.pallas_call(
        functools.partial(_gla_scan_kernel, has_state=has_state, want_state=want_state),
        grid=(st.batch, 2, nb),
        in_specs=in_specs,
        out_specs=out_specs,
        out_shape=out_shape,
        scratch_shapes=[pltpu.VMEM((GLA_HEADS, GLA_DK, GLA_DV), F32)],
        compiler_params=_params("parallel", "parallel", "arbitrary"),
        name="gla_scan",
    )(*args)


def _gla_post_ffn_kernel(o_ref, r_ref, x_ref, mod_ref, gn_ref, go_ref, w_ref, gn2_ref, win_ref, wout_ref, out_ref):
    o = o_ref[0].astype(F32) + o_ref[1].astype(F32)
    heads = []
    for h in range(GLA_HEADS):
        sv = slice(h * GLA_DV, (h + 1) * GLA_DV)
        heads.append(_rms(o[:, sv], go_ref[:, sv]))
    g = (jnp.concatenate(heads, axis=1) * r_ref[...].astype(F32)).astype(BF16)
    x = _residual(x_ref[...], _dot(g, w_ref[...]), mod_ref, gn_ref, 1, 1.0)
    out_ref[...] = _swiglu_sublayer(x, mod_ref, gn2_ref, win_ref, wout_ref, 2)


def _gla_post_ffn(st, o, r, x, mods, gn, g_out, w_out, gn2, w_ffn_in, w_ffn_out, layer):
    return pl.pallas_call(
        _gla_post_ffn_kernel,
        grid=(st.tiles,),
        in_specs=[pl.BlockSpec((2, TOKEN_TILE, GLA_VV), lambda i: (0, i, 0)),
                  st.tok(GLA_VV), st.tok(D_MODEL), st.mod(), _resident((2, D_MODEL)),
                  _resident((1, GLA_VV)), _resident((GLA_VV, D_MODEL)), _resident((2, D_MODEL))]
                 + _ffn_specs(layer, 1),
        out_specs=st.tok(D_MODEL),
        out_shape=jax.ShapeDtypeStruct((st.n, D_MODEL), F32),
        compiler_params=_params("parallel"),
        name="gla_post_ffn",
    )(o, r, x, mods, gn, g_out, w_out, gn2, w_ffn_in, w_ffn_out)


MLA_Q_SCALE = MLA_SCALE * LOG2_E


def _mla_proj_kernel(x_ref, mod_ref, gn_ref, w_ref, gq_ref, gkv_ref, wuqt_ref, tabqt_ref,
                     qt_ref, ckv_ref, kr_ref):
    h = _pre(x_ref[...], mod_ref, gn_ref, 1).astype(BF16)
    p = _dot(h, w_ref[...])
    ckv_ref[...] = _rms(p[:, MLA_Q_RANK:MLA_Q_RANK + MLA_KV_RANK], gkv_ref[...])
    kr_ref[...] = p[:, MLA_Q_RANK + MLA_KV_RANK:]
    cqt = _rms(p[:, 0:MLA_Q_RANK], gq_ref[...]).T.astype(BF16)
    tab = tabqt_ref[...] * MLA_Q_SCALE
    for hd in range(MLA_HEADS):
        qt = _dot(wuqt_ref[hd], cqt)
        for j in range(TOKEN_TILE // MLA_Q_SUB):
            cols = slice(j * MLA_Q_SUB, (j + 1) * MLA_Q_SUB)
            qt_ref[j, hd] = jnp.concatenate([qt[0:MLA_NOPE, cols] * MLA_Q_SCALE, qt[MLA_NOPE:, cols] * tab[:, cols]],
                                            axis=0).astype(BF16)


def _mla_proj(st, x, mods, gn, w_in_ext, g_q, g_kv, w_uqt, tabqt):
    n = st.n
    n_sub = TOKEN_TILE // MLA_Q_SUB
    tab_tiles = tabqt.shape[1] // TOKEN_TILE
    return pl.pallas_call(
        _mla_proj_kernel,
        grid=(st.tiles,),
        in_specs=[st.tok(D_MODEL), st.mod(), _resident((2, D_MODEL)), _resident(w_in_ext.shape),
                  _resident((1, MLA_Q_RANK)), _resident((1, MLA_KV_RANK)), _resident(w_uqt.shape),
                  pl.BlockSpec((2 * MLA_ROPE, TOKEN_TILE), lambda i: (0, i % tab_tiles))],
        out_specs=[pl.BlockSpec((n_sub, MLA_HEADS, MLA_QK_DIM, MLA_Q_SUB), lambda i: (i, 0, 0, 0)),
                   st.tok(MLA_KV_RANK), st.tok(2 * MLA_ROPE)],
        out_shape=[jax.ShapeDtypeStruct((n // MLA_Q_SUB, MLA_HEADS, MLA_QK_DIM, MLA_Q_SUB), BF16),
                   jax.ShapeDtypeStruct((n, MLA_KV_RANK), F32),
                   jax.ShapeDtypeStruct((n, 2 * MLA_ROPE), F32)],
        compiler_params=_params("parallel"),
        name="mla_proj",
    )(x, mods, gn, w_in_ext, g_q, g_kv, w_uqt, tabqt)


def _mla_key_parts(tk):
    half = pl.cdiv(tk // 2, MXU_DIM) * MXU_DIM
    return [(0, tk)] if tk <= MXU_DIM else [(0, half), (half, tk)]


def _mla_build_kv(ckv, kr_dup, wukv, k_dst, vt_dst):
    k_dst[:, MLA_NOPE:] = kr_dup
    for lo, hi in _mla_key_parts(ckv.shape[0]):
        kv = _dot(ckv[lo:hi, :], wukv)
        k_dst[lo:hi, 0:MLA_NOPE] = kv[:, 0:MLA_NOPE].astype(BF16)
        vt_dst[:, lo:hi] = kv[:, MLA_NOPE:].T.astype(BF16)


def _mla_colmax(sts):
    return functools.reduce(jnp.maximum, [jnp.max(s, axis=0, keepdims=True) for s in sts])


def _mla_softmax_pv(sts, m, vt_parts):
    pts = [jnp.exp2(s - m) for s in sts]
    denom = functools.reduce(jnp.add, [jnp.sum(p, axis=0, keepdims=True) for p in pts])
    ot = functools.reduce(jnp.add, [_dot(vt, p.astype(BF16)) for vt, p in zip(vt_parts, pts)])
    return (ot / denom).T.astype(BF16)


def _mla_attn_kernel(qt_ref, ckv_ref, kr_ref, wukv_ref, o_ref, k_scr, vt_scr):
    kr = kr_ref[...]
    kr_dup = (kr + pltpu.roll(kr, MLA_ROPE, axis=1)).astype(BF16)
    ckv = ckv_ref[...]

    def scores(h):
        _mla_build_kv(ckv, kr_dup, wukv_ref[h], k_scr.at[h], vt_scr.at[h])
        return _dot(k_scr[h], qt_ref[h])

    nxt = scores(0)
    for h in range(MLA_HEADS):
        cur, nxt = nxt, (scores(h + 1) if h + 1 < MLA_HEADS else None)
        o_ref[:, h * MLA_V:(h + 1) * MLA_V] = _mla_softmax_pv([cur], _mla_colmax([cur]), [vt_scr[h]])


def _mla_attn(batch, seq, qt, ckv3, kr_masked, w_ukv):
    return pl.pallas_call(
        _mla_attn_kernel,
        grid=(batch,),
        in_specs=[pl.BlockSpec((None, MLA_HEADS, MLA_QK_DIM, seq), lambda b: (b, 0, 0, 0)),
                  pl.BlockSpec((None, seq, MLA_KV_RANK), lambda b: (b, 0, 0)),
                  pl.BlockSpec((None, seq, 2 * MLA_ROPE), lambda b: (b, 0, 0)),
                  _resident(w_ukv.shape)],
        out_specs=pl.BlockSpec((seq, MLA_HEADS * MLA_V), lambda b: (b, 0)),
        out_shape=jax.ShapeDtypeStruct((batch * seq, MLA_HEADS * MLA_V), BF16),
        scratch_shapes=[pltpu.VMEM((MLA_HEADS, seq, MLA_QK_DIM), BF16), pltpu.VMEM((MLA_HEADS, MLA_V, seq), BF16)],
        compiler_params=_params("parallel"),
        name="mla_attn",
    )(qt, ckv3, kr_masked, w_ukv)


def _mla_attn_long_kernel(qt_ref, ckv_ref, kr_ref, tabk_ref, wukv_ref, o_ref,
                          k_scr, vt_scr, st_scr, m_scr, acc_scr, den_scr):
    n_sub = qt_ref.shape[0]
    tk = k_scr.shape[0]
    tiles = [(lo, min(lo + MLA_KEY_TILE, tk)) for lo in range(0, tk, MLA_KEY_TILE)]
    halves = [[(lo, (lo + hi) // 2), ((lo + hi) // 2, hi)] for lo, hi in tiles]

    y = kr_ref[...] * tabk_ref[...]
    _mla_build_kv(ckv_ref[...], (y + pltpu.roll(y, MLA_ROPE, axis=1)).astype(BF16), wukv_ref[...], k_scr, vt_scr)

    def fold_groups(x, op):
        return op(x.reshape(x.shape[0] // SUBLANES, SUBLANES, x.shape[1]), axis=0)

    def finalize(j, slot):
        start = j * MLA_Q_SUB
        rows = pl.ds(start if isinstance(j, int) else pl.multiple_of(start, MLA_Q_SUB), MLA_Q_SUB)
        o_ref[rows, :] = (acc_scr[slot] / jnp.sum(den_scr[slot], axis=0, keepdims=True)).T.astype(BF16)

    def stage(j_next, slot_next, j_cur, slot_cur, j_done):
        qt = None if j_next is None else qt_ref[j_next]
        if j_cur is not None:
            m = jnp.max(m_scr[slot_cur], axis=0, keepdims=True)
            acc = jnp.zeros((MLA_V, MLA_Q_SUB), F32)
            den = jnp.zeros((SUBLANES, MLA_Q_SUB), F32)
        run_max = None
        for n, pieces in enumerate(halves):
            for lo, hi in pieces:
                if qt is not None:
                    s = _dot(k_scr[lo:hi, :], qt)
                    st_scr[slot_next, lo:hi, :] = s
                    smax = fold_groups(s, jnp.max)
                    run_max = smax if run_max is None else jnp.maximum(run_max, smax)
            for lo, hi in pieces:
                if j_cur is not None:
                    p = jnp.exp2(st_scr[slot_cur, lo:hi, :] - m)
                    den = den + fold_groups(p, jnp.sum)
                    acc = acc + _dot(vt_scr[:, lo:hi], p.astype(BF16))
            if n == 0 and j_done is not None:
                finalize(j_done, 1 - slot_cur)
        if qt is not None:
            m_scr[slot_next] = run_max
        if j_cur is not None:
            acc_scr[slot_cur] = acc
            den_scr[slot_cur] = den

    acc_scr[1] = jnp.zeros(acc_scr.shape[1:], F32)
    den_scr[1] = jnp.ones(den_scr.shape[1:], F32)
    stage(0, 0, None, None, None)

    def trip(j):
        for i in range(MLA_UNROLL):
            last = isinstance(j, int) and j + i + 1 == n_sub
            done = max(j + i - 1, 0) if isinstance(j, int) else jnp.maximum(j + i - 1, 0)
            stage(None if last else j + i + 1, (i + 1) % 2, j + i, i % 2, done)

    def body(t, carry):
        trip(MLA_UNROLL * t)
        return carry

    lax.fori_loop(0, n_sub // MLA_UNROLL - 1, body, 0)
    trip(n_sub - MLA_UNROLL)
    finalize(n_sub - 1, (n_sub - 1) % 2)


def _mla_attn_long(batch, tq_total, qt, ckv_all, kr_all, tabk, w_ukv):
    tk = ckv_all.shape[1]
    n_sub = tq_total // MLA_Q_SUB
    once = pl.Buffered(1)
    return pl.pallas_call(
        _mla_attn_long_kernel,
        grid=(batch, MLA_HEADS),
        in_specs=[pl.BlockSpec((n_sub, None, MLA_QK_DIM, MLA_Q_SUB), lambda b, h: (b, h, 0, 0)),
                  pl.BlockSpec((None, tk, MLA_KV_RANK), lambda b, h: (b, 0, 0), pipeline_mode=once),
                  pl.BlockSpec((None, tk, 2 * MLA_ROPE), lambda b, h: (b, 0, 0), pipeline_mode=once),
                  pl.BlockSpec((tk, 2 * MLA_ROPE), lambda b, h: (0, 0), pipeline_mode=once),
                  pl.BlockSpec((None, MLA_KV_RANK, MLA_KV_DIM), lambda b, h: (h, 0, 0))],
        out_specs=pl.BlockSpec((tq_total, MLA_V), lambda b, h: (b, h)),
        out_shape=jax.ShapeDtypeStruct((batch * tq_total, MLA_HEADS * MLA_V), BF16),
        scratch_shapes=[pltpu.VMEM((tk, MLA_QK_DIM), BF16), pltpu.VMEM((MLA_V, tk), BF16),
                        pltpu.VMEM((2, tk, MLA_Q_SUB), F32), pltpu.VMEM((2, SUBLANES, MLA_Q_SUB), F32),
                        pltpu.VMEM((2, MLA_V, MLA_Q_SUB), F32), pltpu.VMEM((2, SUBLANES, MLA_Q_SUB), F32)],
        compiler_params=_params("parallel", "arbitrary"),
        name="mla_attn_long",
    )(qt, ckv_all, kr_all, tabk, w_ukv)


def _mla_post_ffn_kernel(a_ref, x_ref, mod_ref, gn_ref, w_ref, gn2_ref, win_ref, wout_ref, out_ref):
    x = _residual(x_ref[...], _dot(a_ref[...], w_ref[...]), mod_ref, gn_ref, 1, 1.0)
    out_ref[...] = _swiglu_sublayer(x, mod_ref, gn2_ref, win_ref, wout_ref, 2)


def _mla_post_ffn(st, a, x, mods, gn, w_out, gn2, w_ffn_in, w_ffn_out, layer):
    return pl.pallas_call(
        _mla_post_ffn_kernel,
        grid=(st.tiles,),
        in_specs=[st.tok(MLA_HEADS * MLA_V), st.tok(D_MODEL), st.mod(), _resident((2, D_MODEL)),
                  _resident((MLA_HEADS * MLA_V, D_MODEL)), _resident((2, D_MODEL))] + _ffn_specs(layer, 1),
        out_specs=st.tok(D_MODEL),
        out_shape=jax.ShapeDtypeStruct((st.n, D_MODEL), F32),
        compiler_params=_params("parallel"),
        name="mla_post_ffn",
    )(a, x, mods, gn, w_out, gn2, w_ffn_in, w_ffn_out)


_ROPE_SWAP = tuple(list(range(16, 32)) + list(range(0, 16)) + list(range(48, 64)) + list(range(32, 48)))


def _rope_table(n_tokens):
    rows = n_tokens // GRID_W
    row = np.repeat(np.arange(rows), GRID_W).astype(np.float64)
    col = np.tile(np.arange(GRID_W), rows).astype(np.float64)
    n_pairs = MLA_ROPE // 4
    inv = ROPE_BASE ** (-np.arange(n_pairs, dtype=np.float64) / n_pairs)
    ar, ac = row[:, None] * inv, col[:, None] * inv
    cos = np.concatenate([np.cos(ar), np.cos(ar), np.cos(ac), np.cos(ac)], axis=1)
    sin = np.concatenate([-np.sin(ar), np.sin(ar), -np.sin(ac), np.sin(ac)], axis=1)
    return np.concatenate([cos, sin], axis=1).astype(np.float32)


def _identity_table(n_tokens):
    return np.concatenate([np.ones((n_tokens, MLA_ROPE), np.float32), np.zeros((n_tokens, MLA_ROPE), np.float32)], axis=1)


def kernel(x_prompt, x_sample, state_gla, cache_mla_ckv, cache_mla_krope, c, c_ctx, w_mod, b_mod, g_norm, w_ffn_in, w_ffn_out, gla_w_in, gla_w_gate, gla_b_gate, gla_g_out, gla_w_out, mla_w_in, mla_g_q, mla_g_kv, mla_w_uq, mla_w_ukv, mla_w_out):
    batch, seq = x_prompt.shape[0], x_prompt.shape[1]
    dec_batch, dec_seq = x_sample.shape[0], x_sample.shape[1]
    past = cache_mla_ckv.shape[2]
    ctx = _Stream(batch, seq, dec_batch, per_batch_mod=False)
    lat = _Stream(dec_batch, dec_seq, 0, per_batch_mod=True)

    cond8 = jnp.concatenate([c, c_ctx[None, :], jnp.zeros((SUBLANES - dec_batch - 1, D_MODEL), F32)], axis=0)
    mods = _modulation(cond8, w_mod, b_mod)

    xs = {ctx: x_prompt.reshape(ctx.n, D_MODEL), lat: x_sample.reshape(lat.n, D_MODEL)}
    w_ffn_in_b = w_ffn_in.astype(BF16)
    w_ffn_out_b = w_ffn_out.astype(BF16)
    gla_states, mla_ckvs, mla_krs = [], [], []

    for l in range(DEPTH):
        j = l // 2
        for st in (ctx, lat):
            xs[st] = _ffn(st, xs[st], mods[l], g_norm[l, 0], w_ffn_in_b, w_ffn_out_b, l, 0, 0)

        if l % 2 == 0:
            w_in = gla_w_in[j].astype(BF16)
            w_qkvr, w_z = w_in[:, :2 * GLA_QK + 2 * GLA_VV], w_in[:, 2 * GLA_QK + 2 * GLA_VV:]
            zeros = jnp.zeros((GLA_GATE_RANK, GLA_QK), F32)
            w_gate_bd = jnp.concatenate([jnp.concatenate([gla_w_gate[j, 0], zeros], axis=1),
                                         jnp.concatenate([zeros, gla_w_gate[j, 1]], axis=1)], axis=0).astype(BF16)
            b_gate = gla_b_gate[j].reshape(1, 2 * GLA_QK)
            g_out = gla_g_out[j].reshape(1, GLA_VV)
            w_out = gla_w_out[j].astype(BF16)
            for st in (ctx, lat):
                q, k, v, r, la = _gla_proj(st, xs[st], mods[l], g_norm[l, 1], w_qkvr, w_z, w_gate_bd, b_gate)
                if st is ctx:
                    o, s_fin = _gla_scan(st, q, k, v, la, None, True)
                    gla_states.append(s_fin)
                else:
                    (o,) = _gla_scan(st, q, k, v, la, state_gla[:, j].astype(F32), False)
                xs[st] = _gla_post_ffn(st, o, r, xs[st], mods[l], g_norm[l, 1], g_out, w_out,
                                       g_norm[l, 2], w_ffn_in_b, w_ffn_out_b, l)
        else:
            swap = jnp.array(_ROPE_SWAP)
            w_in = mla_w_in[j]
            w_in_ext = jnp.concatenate([w_in, w_in[:, MLA_Q_RANK + MLA_KV_RANK:][:, swap]], axis=1).astype(BF16)
            w_uq = mla_w_uq[j].reshape(MLA_Q_RANK, MLA_HEADS, MLA_NOPE + MLA_ROPE)
            w_uq = jnp.concatenate([w_uq, w_uq[:, :, MLA_NOPE:][:, :, swap]], axis=2)
            w_uqt = jnp.transpose(w_uq, (1, 2, 0)).astype(BF16)
            w_ukv = jnp.transpose(mla_w_ukv[j].reshape(MLA_KV_RANK, MLA_HEADS, MLA_KV_DIM), (1, 0, 2)).astype(BF16)
            g_q = mla_g_q[j].reshape(1, MLA_Q_RANK)
            g_kv = mla_g_kv[j].reshape(1, MLA_KV_RANK)
            w_out = mla_w_out[j].astype(BF16)
            tabq = _rope_table(lat.seq)
            tabqt = {ctx: jnp.asarray(_identity_table(TOKEN_TILE).T), lat: jnp.asarray(tabq.T)}
            for st in (ctx, lat):
                qt, ckv, kr2 = _mla_proj(st, xs[st], mods[l], g_norm[l, 1], w_in_ext, g_q, g_kv, w_uqt, tabqt[st])
                ckv3 = ckv.reshape(st.batch, st.seq, MLA_KV_RANK)
                kr3 = kr2.reshape(st.batch, st.seq, 2 * MLA_ROPE)
                if st is ctx:
                    mla_ckvs.append(ckv3)
                    kr = kr3[:, :, :MLA_ROPE]
                    mla_krs.append(kr)
                    kr_masked = jnp.concatenate([kr, jnp.zeros_like(kr)], axis=2)
                    a = _mla_attn(st.batch, st.seq, qt, ckv3.astype(BF16), kr_masked, w_ukv)
                else:
                    ckv_all = jnp.concatenate([cache_mla_ckv[:, j].astype(F32), ckv3], axis=1).astype(BF16)
                    kr_ctx = cache_mla_krope[:, j].astype(F32)
                    kr_all = jnp.concatenate([jnp.concatenate([kr_ctx, jnp.zeros_like(kr_ctx)], axis=2), kr3], axis=1)
                    tabk = jnp.asarray(np.concatenate([_identity_table(past), tabq], axis=0))
                    a = _mla_attn_long(st.batch, st.seq, qt, ckv_all, kr_all, tabk, w_ukv)
                xs[st] = _mla_post_ffn(st, a, xs[st], mods[l], g_norm[l, 1], w_out,
                                       g_norm[l, 2], w_ffn_in_b, w_ffn_out_b, l)

    yp = xs[ctx].reshape(batch, seq, D_MODEL)
    ys = xs[lat].reshape(dec_batch, dec_seq, D_MODEL)
    new_state_gla = jnp.stack(gla_states, axis=1).astype(x_prompt.dtype)
    new_cache_mla_ckv = jnp.stack(mla_ckvs, axis=1)
    new_cache_mla_krope = jnp.stack(mla_krs, axis=1)
    return (yp, ys, new_state_gla, new_cache_mla_ckv, new_cache_mla_krope)
```
